```python
import jax, jax.numpy as jnp
from jax import lax
import numpy as np

D_MODEL = 1024
BATCH = 16
SEQ = 256
DEPTH = 4
DEC_BATCH = 8
DEC_SEQ = 4096
PAST_LEN = 512

GRID_W = 64
D_POOL = 512
N_POOL_GROUPS = 4
POOL_GROUP_W = D_POOL // N_POOL_GROUPS
POOL_WINDOWS = (2, 4, 8, 16)
D_LRU = 512
N_LRU_HEADS = 8
LRU_HEAD_W = D_LRU // N_LRU_HEADS
CONV_W = 4
LRU_C = 8.0
D_MIX = D_POOL + D_LRU
D_IN = D_POOL + 2 * D_LRU
D_FF = 2816
N_MOD = 9
EPS = 1e-6

kernel_name = 'hymba_pool_rglru_macaron_prefix_dit'


def rms_norm(x, g):
    xf = x.astype(jnp.float32)
    y = xf * lax.rsqrt(jnp.mean(xf * xf, axis=-1, keepdims=True) + EPS)
    return (y * g.astype(jnp.float32)).astype(x.dtype)


def modulate(x, shift, scale):
    return x * (1 + scale) + shift


def swiglu(x, w_gate, w_up, w_down):
    return (jax.nn.silu(x @ w_gate) * (x @ w_up)) @ w_down


def multiscale_pool(u, w_pool, pool_scale):
    n, L, _ = u.shape
    uf = u.astype(jnp.float32)
    cs = jnp.concatenate([jnp.zeros((n, 1, D_POOL), jnp.float32), jnp.cumsum(uf, axis=1)], axis=1)
    t = jnp.arange(L)
    outs = []
    for g, w in enumerate(POOL_WINDOWS):
        lo = jnp.clip(t - w // 2, 0, L)
        hi = jnp.clip(t + w // 2, 0, L)
        sl = slice(g * POOL_GROUP_W, (g + 1) * POOL_GROUP_W)
        csg = cs[..., sl]
        s = jnp.take(csg, hi, axis=1) - jnp.take(csg, lo, axis=1)
        cnt = (hi - lo).astype(jnp.float32)[None, :, None]
        outs.append(s / cnt - uf[..., sl])
    pooled = jnp.stack(outs, axis=2).astype(u.dtype)
    mixed = jnp.einsum('nlgc,gcd->nlgd', pooled, w_pool).reshape(n, L, D_POOL)
    return mixed * pool_scale


def centred_conv(u, w, b):
    L = u.shape[1]
    up = jnp.pad(u, ((0, 0), (CONV_W // 2, CONV_W - 1 - CONV_W // 2), (0, 0)))
    out = b + up[:, 0:L] * w[0]
    for k in range(1, CONV_W):
        out = out + up[:, k:k + L] * w[k]
    return out


def block_diag(x, w, b):
    bsz, L, _ = x.shape
    y = jnp.einsum('blhi,hij->blhj', x.reshape(bsz, L, N_LRU_HEADS, LRU_HEAD_W), w)
    return y.reshape(bsz, L, D_LRU) + b


def rglru(x, w_r, b_r, w_i, b_i, lam, h0, reverse):
    xf = x.astype(jnp.float32)
    r = jax.nn.sigmoid(block_diag(xf, w_r.astype(jnp.float32), b_r.astype(jnp.float32)))
    i = jax.nn.sigmoid(block_diag(xf, w_i.astype(jnp.float32), b_i.astype(jnp.float32)))
    log_a = LRU_C * r * jax.nn.log_sigmoid(lam.astype(jnp.float32))
    a = jnp.exp(log_a)
    bv = jnp.sqrt(-jnp.expm1(2.0 * log_a)) * (i * xf)
    idx = -1 if reverse else 0
    bv = bv.at[:, idx].add(a[:, idx] * h0.astype(jnp.float32))

    def combine(e1, e2):
        a1, b1 = e1
        a2, b2 = e2
        return a1 * a2, a2 * b1 + b2

    _, h = lax.associative_scan(combine, (a, bv), axis=1, reverse=reverse)
    last = 0 if reverse else -1
    return h, h[:, last]


def mixer(h, h0, on_grid, w_in, conv_w, conv_b, w_pool, pool_scale, w_r, b_r, w_i, b_i, lam, w_out):
    bsz, L, _ = h.shape
    u = h @ w_in
    u_pool = u[..., :D_POOL]
    u_rec = u[..., D_POOL:D_POOL + D_LRU]
    u_gate = u[..., D_POOL + D_LRU:]
    if on_grid:
        rows = L // GRID_W
        pool_out = multiscale_pool(u_pool.reshape(bsz * rows, GRID_W, D_POOL), w_pool, pool_scale)
        pool_out = pool_out.reshape(bsz, L, D_POOL)
    else:
        pool_out = multiscale_pool(u_pool, w_pool, pool_scale)
    xc = centred_conv(u_rec, conv_w, conv_b)
    y_f, s_f = rglru(xc, w_r[0], b_r[0], w_i[0], b_i[0], lam[0], h0[:, 0], False)
    y_b, s_b = rglru(xc, w_r[1], b_r[1], w_i[1], b_i[1], lam[1], h0[:, 1], True)
    rec_out = (y_f + y_b).astype(h.dtype) * jax.nn.gelu(u_gate)
    out = jnp.concatenate([pool_out, rec_out], axis=-1) @ w_out
    return out, jnp.stack([s_f, s_b], axis=1).astype(h.dtype)


def run_layer(x, mod, h0, on_grid, norm_g, f1g, f1u, f1d, w_in, conv_w, conv_b, w_pool, pool_scale,
              w_r, b_r, w_i, b_i, lam, w_out, f2g, f2u, f2d):
    m = [mod[:, k][:, None, :] for k in range(N_MOD)]
    h = modulate(rms_norm(x, norm_g[0]), m[0], m[1])
    x = x + 0.5 * m[2] * swiglu(h, f1g, f1u, f1d)
    h = modulate(rms_norm(x, norm_g[1]), m[3], m[4])
    y, st = mixer(h, h0, on_grid, w_in, conv_w, conv_b, w_pool, pool_scale, w_r, b_r, w_i, b_i, lam, w_out)
    x = x + m[5] * y
    h = modulate(rms_norm(x, norm_g[2]), m[6], m[7])
    x = x + 0.5 * m[8] * swiglu(h, f2g, f2u, f2d)
    return x, st


def setup_inputs(seed: int = 0) -> dict:
    key = jax.random.key(seed)
    ks = jax.random.split(key, 32)
    f32 = jnp.float32
    nrm = lambda k, shape, s: jax.random.normal(k, shape, f32) * s
    u_a = jax.random.uniform(ks[20], (DEPTH, 2, D_LRU), f32, 0.9, 0.999)
    s_a = u_a ** (1.0 / LRU_C)
    lam = jnp.log(s_a) - jnp.log1p(-s_a)
    return {
        'x_prompt': nrm(ks[0], (BATCH, SEQ, D_MODEL), 1.0),
        'x_sample': nrm(ks[1], (DEC_BATCH, DEC_SEQ, D_MODEL), 1.0),
        'state_lru': nrm(ks[2], (DEC_BATCH, DEPTH, 2, D_LRU), 0.5),
        'c': nrm(ks[3], (DEC_BATCH, D_MODEL), 1.0),
        'c_ctx': nrm(ks[4], (D_MODEL,), 1.0),
        'norm_g': 1.0 + nrm(ks[5], (DEPTH, 3, D_MODEL), 0.05),
        'w_ada': nrm(ks[6], (DEPTH, D_MODEL, N_MOD * D_MODEL), 0.5 * D_MODEL ** -0.5),
        'b_ada': nrm(ks[7], (DEPTH, N_MOD * D_MODEL), 0.02),
        'ffn1_gate': nrm(ks[8], (DEPTH, D_MODEL, D_FF), D_MODEL ** -0.5),
        'ffn1_up': nrm(ks[9], (DEPTH, D_MODEL, D_FF), D_MODEL ** -0.5),
        'ffn1_down': nrm(ks[10], (DEPTH, D_FF, D_MODEL), D_FF ** -0.5),
        'w_in': nrm(ks[11], (DEPTH, D_MODEL, D_IN), D_MODEL ** -0.5),
        'conv_w': nrm(ks[12], (DEPTH, CONV_W, D_LRU), CONV_W ** -0.5),
        'conv_b': nrm(ks[13], (DEPTH, D_LRU), 0.02),
        'w_pool': nrm(ks[14], (DEPTH, N_POOL_GROUPS, POOL_GROUP_W, POOL_GROUP_W), POOL_GROUP_W ** -0.5),
        'pool_scale': 1.0 + nrm(ks[15], (DEPTH, D_POOL), 0.05),
        'lru_w_r': nrm(ks[16], (DEPTH, 2, N_LRU_HEADS, LRU_HEAD_W, LRU_HEAD_W), LRU_HEAD_W ** -0.5),
        'lru_b_r': nrm(ks[17], (DEPTH, 2, D_LRU), 0.02),
        'lru_w_i': nrm(ks[18], (DEPTH, 2, N_LRU_HEADS, LRU_HEAD_W, LRU_HEAD_W), LRU_HEAD_W ** -0.5),
        'lru_b_i': nrm(ks[19], (DEPTH, 2, D_LRU), 0.02),
        'lru_lambda': lam,
        'w_out': nrm(ks[21], (DEPTH, D_MIX, D_MODEL), D_MIX ** -0.5),
        'ffn2_gate': nrm(ks[22], (DEPTH, D_MODEL, D_FF), D_MODEL ** -0.5),
        'ffn2_up': nrm(ks[23], (DEPTH, D_MODEL, D_FF), D_MODEL ** -0.5),
        'ffn2_down': nrm(ks[24], (DEPTH, D_FF, D_MODEL), D_FF ** -0.5),
        'final_g': 1.0 + nrm(ks[25], (D_MODEL,), 0.05),
    }


def reference(x_prompt, x_sample, state_lru, c, c_ctx, norm_g, w_ada, b_ada, ffn1_gate, ffn1_up, ffn1_down,
              w_in, conv_w, conv_b, w_pool, pool_scale, lru_w_r, lru_b_r, lru_w_i, lru_b_i, lru_lambda,
              w_out, ffn2_gate, ffn2_up, ffn2_down, final_g):
    xp = x_prompt
    xs = x_sample
    h0_ctx = jnp.zeros((xp.shape[0], 2, D_LRU), xp.dtype)
    sc_ctx = jax.nn.silu(c_ctx)[None, :]
    sc_lat = jax.nn.silu(c)
    new_states = []
    for l in range(DEPTH):
        mod_ctx = (sc_ctx @ w_ada[l] + b_ada[l]).reshape(1, N_MOD, D_MODEL)
        mod_lat = (sc_lat @ w_ada[l] + b_ada[l]).reshape(sc_lat.shape[0], N_MOD, D_MODEL)
        lp = (norm_g[l], ffn1_gate[l], ffn1_up[l], ffn1_down[l], w_in[l], conv_w[l], conv_b[l],
              w_pool[l], pool_scale[l], lru_w_r[l], lru_b_r[l], lru_w_i[l], lru_b_i[l], lru_lambda[l],
              w_out[l], ffn2_gate[l], ffn2_up[l], ffn2_down[l])
        xp, st = run_layer(xp, mod_ctx, h0_ctx, False, *lp)
        new_states.append(st)
        xs, _ = run_layer(xs, mod_lat, state_lru[:, l], True, *lp)
    y_prompt = rms_norm(xp, final_g)
    y_sample = rms_norm(xs, final_g)
    new_state_lru = jnp.stack(new_states, axis=1)
    return (y_prompt, y_sample, new_state_lru)
```

```python
import functools

import numpy as np
import jax
import jax.numpy as jnp
from jax import lax
from jax.experimental import pallas as pl
from jax.experimental.pallas import tpu as pltpu

D_MODEL = 1024
DEPTH = 4
GRID_W = 64
D_POOL = 512
N_POOL_GROUPS = 4
POOL_GROUP_W = D_POOL // N_POOL_GROUPS
POOL_WINDOWS = (2, 4, 8, 16)
POOL_HALO = max(POOL_WINDOWS) // 2
D_LRU = 512
N_LRU_HEADS = 8
LRU_HEAD_W = D_LRU // N_LRU_HEADS
CONV_W = 4
CONV_BACK = CONV_W // 2
CONV_FWD = CONV_W - 1 - CONV_BACK
LRU_C = 8.0
D_MIX = D_POOL + D_LRU
D_IN = D_POOL + 2 * D_LRU
D_FF = 2816
N_MOD = 9
EPS = 1e-6

MXU_N = 256
FF_CHUNK = MXU_N
N_FF_CHUNKS = D_FF // FF_CHUNK
MOD_ROWS = 16
ADA_NB = 1152
TOKEN_TILE = 512
VMEM_LIMIT = 56 * 1024 * 1024

BF16 = jnp.bfloat16
F32 = jnp.float32


def _dot(a, b):
    return jnp.dot(a, b, preferred_element_type=F32)


def _rms_norm(x, g):
    ms = jnp.mean(x * x, axis=-1, keepdims=True)
    return x * lax.rsqrt(ms + EPS) * g


def _per_batch(x, bsz, fn):
    n, d = x.shape
    return fn(x.reshape(n // bsz, bsz, d)).reshape(n, d)


def _params(sem):
    return pltpu.CompilerParams(dimension_semantics=sem, vmem_limit_bytes=VMEM_LIMIT)


def _ada_body(c_ref, w_ref, b_ref, o_ref):
    c = c_ref[...]
    s = (c * jax.nn.sigmoid(c)).astype(BF16)
    o_ref[...] = _dot(s, w_ref[...].astype(BF16)) + b_ref[...]


def _ada_call(cmat, w_ada, b_ada):
    nb = (N_MOD * D_MODEL) // ADA_NB
    return pl.pallas_call(
        _ada_body,
        grid=(DEPTH, nb),
        in_specs=[
            pl.BlockSpec((MOD_ROWS, D_MODEL), lambda l, j: (0, 0)),
            pl.BlockSpec((None, D_MODEL, ADA_NB), lambda l, j: (l, 0, j)),
            pl.BlockSpec((None, 1, ADA_NB), lambda l, j: (l, 0, j)),
        ],
        out_specs=pl.BlockSpec((None, MOD_ROWS, ADA_NB), lambda l, j: (l, 0, j)),
        out_shape=jax.ShapeDtypeStruct((DEPTH, MOD_ROWS, N_MOD * D_MODEL), F32),
        compiler_params=_params(("arbitrary", "arbitrary")),
        name="adaln",
    )(cmat, w_ada, b_ada.reshape(DEPTH, 1, N_MOD * D_MODEL))


def _ffn_body(bsz, final, x_ref, mod_ref, g_ref, wg_ref, wu_ref, wd_ref, *rest):
    if final:
        fg_ref, o_ref, acc_ref = rest
    else:
        o_ref, acc_ref = rest
    x = x_ref[...]
    xn = _rms_norm(x, g_ref[...])
    shift, scale, gate = mod_ref[0], mod_ref[1], mod_ref[2]
    hb = _per_batch(xn, bsz, lambda v: v * (1.0 + scale)[None] + shift[None]).astype(BF16)
    for c in range(N_FF_CHUNKS):
        sl = slice(c * FF_CHUNK, (c + 1) * FF_CHUNK)
        g = _dot(hb, wg_ref[:, sl])
        u = _dot(hb, wu_ref[:, sl])
        a = ((g * jax.nn.sigmoid(g)) * u).astype(BF16)
        part = _dot(a, wd_ref[sl, :])
        if c == 0:
            acc_ref[...] = part
        else:
            acc_ref[...] += part
    out = x + _per_batch(acc_ref[...], bsz, lambda v: v * (0.5 * gate)[None])
    if final:
        out = _rms_norm(out, fg_ref[...])
    o_ref[...] = out


def _ffn_call(x, mod, norm_g, wg, wu, wd, final_g, *, layer, sub, bsz):
    n = x.shape[0]
    tm = TOKEN_TILE
    final = final_g is not None
    in_specs = [
        pl.BlockSpec((tm, D_MODEL), lambda i: (i, 0)),
        pl.BlockSpec((None, None, 3, bsz, D_MODEL), lambda i: (layer, sub, 0, 0, 0)),
        pl.BlockSpec((None, 1, D_MODEL), lambda i: (3 * layer + sub, 0, 0)),
        pl.BlockSpec((None, D_MODEL, D_FF), lambda i: (layer, 0, 0)),
        pl.BlockSpec((None, D_MODEL, D_FF), lambda i: (layer, 0, 0)),
        pl.BlockSpec((None, D_FF, D_MODEL), lambda i: (layer, 0, 0)),
    ]
    args = [x, mod, norm_g, wg, wu, wd]
    if final:
        in_specs.append(pl.BlockSpec((1, D_MODEL), lambda i: (0, 0)))
        args.append(final_g)
    return pl.pallas_call(
        functools.partial(_ffn_body, bsz, final),
        grid=(n // tm,),
        in_specs=in_specs,
        out_specs=pl.BlockSpec((tm, D_MODEL), lambda i: (i, 0)),
        out_shape=jax.ShapeDtypeStruct((n, D_MODEL), F32),
        scratch_shapes=[pltpu.VMEM((tm, D_MODEL), F32)],
        compiler_params=_params(("parallel",)),
        name="ffn",
    )(*args)


def _inproj_body(bsz, x_ref, mod_ref, g_ref, w_ref, up_ref, ur_ref, gg_ref):
    xn = _rms_norm(x_ref[...], g_ref[...])
    shift, scale = mod_ref[0], mod_ref[1]
    hb = _per_batch(xn, bsz, lambda v: v * (1.0 + scale)[None] + shift[None]).astype(BF16)
    u = _dot(hb, w_ref[...])
    up_ref[...] = u[:, :D_POOL]
    ur_ref[...] = u[:, D_POOL:D_POOL + D_LRU]
    gg_ref[...] = jax.nn.gelu(u[:, D_POOL + D_LRU:], approximate=True).astype(BF16)


def _inproj_call(x, mod, norm_g, w_in, *, layer, bsz):
    n = x.shape[0]
    tm = TOKEN_TILE
    return pl.pallas_call(
        functools.partial(_inproj_body, bsz),
        grid=(n // tm,),
        in_specs=[
            pl.BlockSpec((tm, D_MODEL), lambda i: (i, 0)),
            pl.BlockSpec((None, None, 3, bsz, D_MODEL), lambda i: (layer, 1, 0, 0, 0)),
            pl.BlockSpec((None, 1, D_MODEL), lambda i: (3 * layer + 1, 0, 0)),
            pl.BlockSpec((None, D_MODEL, D_IN), lambda i: (layer, 0, 0)),
        ],
        out_specs=[
            pl.BlockSpec((tm, D_POOL), lambda i: (i, 0)),
            pl.BlockSpec((tm, D_LRU), lambda i: (i, 0)),
            pl.BlockSpec((tm, D_LRU), lambda i: (i, 0)),
        ],
        out_shape=[
            jax.ShapeDtypeStruct((n, D_POOL), F32),
            jax.ShapeDtypeStruct((n, D_LRU), F32),
            jax.ShapeDtypeStruct((n, D_LRU), BF16),
        ],
        compiler_params=_params(("parallel",)),
        name="inproj",
    )(x, mod, norm_g, w_in)


def _pool_sums(e, bsz, tl):
    p2 = e[0:(tl + 15) * bsz] + e[bsz:(tl + 16) * bsz]
    p4 = p2[0:(tl + 13) * bsz] + p2[2 * bsz:(tl + 15) * bsz]
    p8 = p4[0:(tl + 9) * bsz] + p4[4 * bsz:(tl + 13) * bsz]
    p16 = p8[0:tl * bsz] + p8[8 * bsz:(tl + 8) * bsz]
    n = tl * bsz
    return {2: p2[7 * bsz:7 * bsz + n], 4: p4[6 * bsz:6 * bsz + n], 8: p8[4 * bsz:4 * bsz + n], 16: p16}


def _seq_body(bsz, tl, pool_halo, *refs):
    it = iter(refs)
    up_ref = next(it)
    if pool_halo:
        up_prev_ref, up_next_ref = next(it), next(it)
    urf_ref, urf_prev_ref, urf_next_ref = next(it), next(it), next(it)
    urb_ref, urb_prev_ref, urb_next_ref = next(it), next(it), next(it)
    icnt_ref, cw_ref, cb_ref, wp_ref, ps_ref, wgt_ref, bgt_ref, lam_ref, h0_ref = (next(it) for _ in range(9))
    pm_ref, yf_ref, yb_ref, st_ref = (next(it) for _ in range(4))
    extp_ref, extc_ref, a_ref, b_ref, h_ref = (next(it) for _ in range(5))

    j = pl.program_id(0)
    nj = pl.num_programs(0)
    n = tl * bsz
    ph = POOL_HALO * bsz

    if pool_halo:
        extp_ref[0:ph] = jnp.where(j > 0, up_prev_ref[...], 0.0)
        extp_ref[ph + n:ph + n + ph] = jnp.where(j < nj - 1, up_next_ref[...], 0.0)
    else:
        zero = jnp.zeros((ph, D_POOL), F32)
        extp_ref[0:ph] = zero
        extp_ref[ph + n:ph + n + ph] = zero
    extp_ref[ph:ph + n] = up_ref[...]
    pooled = []
    for g, w in enumerate(POOL_WINDOWS):
        sl = slice(g * POOL_GROUP_W, (g + 1) * POOL_GROUP_W)
        e = extp_ref[:, sl]
        s = _pool_sums(e, bsz, tl)[w]
        pooled.append(s * icnt_ref[:, sl] - e[ph:ph + n])
    pooled = jnp.concatenate(pooled, axis=-1).astype(BF16)
    mixed = [_dot(pooled[:, k * MXU_N:(k + 1) * MXU_N], wp_ref[k]) for k in range(D_POOL // MXU_N)]
    pm_ref[...] = (jnp.concatenate(mixed, axis=-1) * ps_ref[...]).astype(BF16)

    cb_rows = CONV_BACK * bsz
    cf_rows = CONV_FWD * bsz
    for d, (main, prev, nxt) in enumerate(((urf_ref, urf_prev_ref, urf_next_ref),
                                           (urb_ref, urb_prev_ref, urb_next_ref))):
        tile = j if d == 0 else nj - 1 - j
        extc_ref[0:cb_rows] = jnp.where(tile > 0, prev[...], 0.0)
        extc_ref[cb_rows:cb_rows + n] = main[...]
        extc_ref[cb_rows + n:cb_rows + n + cf_rows] = jnp.where(tile < nj - 1, nxt[...], 0.0)
        xc = cb_ref[...] + extc_ref[0:n] * cw_ref[0:1]
        for k in range(1, CONV_W):
            xc = xc + extc_ref[k * bsz:k * bsz + n] * cw_ref[k:k + 1]
        xcb = xc.astype(BF16)
        for kb in range(D_LRU // MXU_N):
            sl = slice(kb * MXU_N, (kb + 1) * MXU_N)
            z = _dot(xcb[:, sl], wgt_ref[d, kb]) + bgt_ref[d, kb]
            r = jax.nn.sigmoid(z[:, :MXU_N])
            i = jax.nn.sigmoid(z[:, MXU_N:])
            log_a = LRU_C * r * jax.nn.log_sigmoid(lam_ref[d][:, sl])
            a = jnp.exp(log_a)
            a_ref[d, :, sl] = a
            b_ref[d, :, sl] = jnp.sqrt(-jnp.tanh(log_a) * (a * a + 1.0)) * (i * xc[:, sl])

    @pl.when(j == 0)
    def _():
        h_ref[...] = h0_ref[...]

    def step(t, carry):
        hf, hb = carry
        rf = pl.ds(pl.multiple_of(t * bsz, bsz), bsz)
        rb = pl.ds(pl.multiple_of((tl - 1 - t) * bsz, bsz), bsz)
        hf = a_ref[0, rf, :] * hf + b_ref[0, rf, :]
        hb = a_ref[1, rb, :] * hb + b_ref[1, rb, :]
        yf_ref[rf, :] = hf
        yb_ref[rb, :] = hb
        return hf, hb

    hf, hb = lax.fori_loop(0, tl, step, (h_ref[0], h_ref[1]), unroll=8)
    h_ref[0] = hf
    h_ref[1] = hb

    @pl.when(j == nj - 1)
    def _():
        st_ref[0] = hf
        st_ref[1] = hb


def _seq_call(up, ur, icnt, conv_w, conv_b, wp, ps, wgt, bgt, lam, h0, *, layer, bsz, tl, pool_halo):
    n_rows = up.shape[0]
    n = tl * bsz
    nt = n_rows // n
    last_t = n_rows // bsz - 1
    hp = POOL_HALO * bsz
    in_specs = [pl.BlockSpec((n, D_POOL), lambda j: (j, 0))]
    args = [up]
    if pool_halo:
        in_specs += [
            pl.BlockSpec((hp, D_POOL), lambda j: (jnp.maximum(j * (tl // POOL_HALO) - 1, 0), 0)),
            pl.BlockSpec((hp, D_POOL), lambda j: (jnp.minimum((j + 1) * (tl // POOL_HALO), n_rows // hp - 1), 0)),
        ]
        args += [up, up]
    for tile in (lambda j: j, lambda j: nt - 1 - j):
        in_specs += [
            pl.BlockSpec((n, D_LRU), lambda j, tile=tile: (tile(j), 0)),
            pl.BlockSpec((CONV_BACK * bsz, D_LRU),
                         lambda j, tile=tile: (jnp.maximum(tile(j) * (tl // CONV_BACK) - 1, 0), 0)),
            pl.BlockSpec((CONV_FWD * bsz, D_LRU),
                         lambda j, tile=tile: (jnp.minimum((tile(j) + 1) * (tl // CONV_FWD), last_t // CONV_FWD), 0)),
        ]
        args += [ur, ur, ur]
    icnt_map = (lambda j: (j, 0)) if pool_halo else (lambda j: (0, 0))
    in_specs += [
        pl.BlockSpec((n, D_POOL), icnt_map),
        pl.BlockSpec((None, CONV_W, D_LRU), lambda j: (layer, 0, 0)),
        pl.BlockSpec((None, 1, D_LRU), lambda j: (layer, 0, 0)),
        pl.BlockSpec((None, D_POOL // MXU_N, MXU_N, MXU_N), lambda j: (layer, 0, 0, 0)),
        pl.BlockSpec((None, 1, D_POOL), lambda j: (layer, 0, 0)),
        pl.BlockSpec((None, 2, D_LRU // MXU_N, MXU_N, 2 * MXU_N), lambda j: (layer, 0, 0, 0, 0)),
        pl.BlockSpec((None, 2, D_LRU // MXU_N, 1, 2 * MXU_N), lambda j: (layer, 0, 0, 0, 0)),
        pl.BlockSpec((None, 2, 1, D_LRU), lambda j: (layer, 0, 0, 0)),
        pl.BlockSpec((2, bsz, D_LRU), lambda j: (0, 0, 0)),
    ]
    args += [icnt, conv_w, conv_b, wp, ps, wgt, bgt, lam, h0]
    return pl.pallas_call(
        functools.partial(_seq_body, bsz, tl, pool_halo),
        grid=(nt,),
        in_specs=in_specs,
        out_specs=[
            pl.BlockSpec((n, D_POOL), lambda j: (j, 0)),
            pl.BlockSpec((n, D_LRU), lambda j: (j, 0)),
            pl.BlockSpec((n, D_LRU), lambda j: (nt - 1 - j, 0)),
            pl.BlockSpec((2, bsz, D_LRU), lambda j: (0, 0, 0)),
        ],
        out_shape=[
            jax.ShapeDtypeStruct((n_rows, D_POOL), BF16),
            jax.ShapeDtypeStruct((n_rows, D_LRU), F32),
            jax.ShapeDtypeStruct((n_rows, D_LRU), F32),
            jax.ShapeDtypeStruct((2, bsz, D_LRU), F32),
        ],
        scratch_shapes=[
            pltpu.VMEM(((tl + 2 * POOL_HALO) * bsz, D_POOL), F32),
            pltpu.VMEM(((tl + CONV_W - 1) * bsz, D_LRU), F32),
            pltpu.VMEM((2, n, D_LRU), F32),
            pltpu.VMEM((2, n, D_LRU), F32),
            pltpu.VMEM((2, bsz, D_LRU), F32),
        ],
        compiler_params=_params(("arbitrary",)),
        name="seqmix",
    )(*args)


def _outproj_body(bsz, x_ref, mod_ref, pm_ref, gg_ref, yf_ref, yb_ref, w_ref, o_ref):
    rec = ((yf_ref[...] + yb_ref[...]) * gg_ref[...].astype(F32)).astype(BF16)
    y = _dot(jnp.concatenate([pm_ref[...], rec], axis=-1), w_ref[...])
    gate = mod_ref[0]
    o_ref[...] = x_ref[...] + _per_batch(y, bsz, lambda v: v * gate[None])


def _outproj_call(x, mod, pm, gg, yf, yb, w_out, *, layer, bsz):
    n = x.shape[0]
    tm = TOKEN_TILE
    row = lambda width: pl.BlockSpec((tm, width), lambda i: (i, 0))
    return pl.pallas_call(
        functools.partial(_outproj_body, bsz),
        grid=(n // tm,),
        in_specs=[
            row(D_MODEL),
            pl.BlockSpec((None, None, 1, bsz, D_MODEL), lambda i: (layer, 1, 2, 0, 0)),
            row(D_POOL), row(D_LRU), row(D_LRU), row(D_LRU),
            pl.BlockSpec((None, D_MIX, D_MODEL), lambda i: (layer, 0, 0)),
        ],
        out_specs=row(D_MODEL),
        out_shape=jax.ShapeDtypeStruct((n, D_MODEL), F32),
        compiler_params=_params(("parallel",)),
        name="outproj",
    )(x, mod, pm, gg, yf, yb, w_out)


def _inv_counts(n_pos, bsz):
    t = np.arange(n_pos)
    cols = []
    for w in POOL_WINDOWS:
        cnt = np.clip(t + w // 2, 0, n_pos) - np.clip(t - w // 2, 0, n_pos)
        cols.append(np.repeat((1.0 / cnt)[:, None], POOL_GROUP_W, axis=1))
    tab = np.concatenate(cols, axis=1).astype(np.float32)
    return jnp.asarray(np.repeat(tab, bsz, axis=0))


def _block_diag(w):
    heads, hw = w.shape[-3], w.shape[-1]
    eye = jnp.eye(heads, dtype=w.dtype)
    full = jnp.einsum('...hij,hg->...higj', w, eye)
    return full.reshape(w.shape[:-3] + (heads * hw, heads * hw))


def kernel(x_prompt, x_sample, state_lru, c, c_ctx, norm_g, w_ada, b_ada, ffn1_gate, ffn1_up, ffn1_down,
           w_in, conv_w, conv_b, w_pool, pool_scale, lru_w_r, lru_b_r, lru_w_i, lru_b_i, lru_lambda,
           w_out, ffn2_gate, ffn2_up, ffn2_down, final_g):
    b_ctx, l_ctx, _ = x_prompt.shape
    b_lat, l_lat, _ = x_sample.shape

    cmat = jnp.zeros((MOD_ROWS, D_MODEL), F32).at[:b_lat].set(c).at[b_lat].set(c_ctx)
    mod = _ada_call(cmat, w_ada, b_ada).reshape(DEPTH, MOD_ROWS, N_MOD, D_MODEL)
    mod_lat = jnp.transpose(mod[:, :b_lat], (0, 2, 1, 3)).reshape(DEPTH, 3, 3, b_lat, D_MODEL)
    mod_ctx = jnp.broadcast_to(mod[:, b_lat][:, :, None, :], (DEPTH, N_MOD, b_ctx, D_MODEL))
    mod_ctx = mod_ctx.reshape(DEPTH, 3, 3, b_ctx, D_MODEL)

    bf = lambda w: w.astype(BF16)
    f1g, f1u, f1d, f2g, f2u, f2d = (bf(w) for w in (ffn1_gate, ffn1_up, ffn1_down, ffn2_gate, ffn2_up, ffn2_down))
    w_in_b, w_out_b = bf(w_in), bf(w_out)
    wp = bf(_block_diag(w_pool.reshape(DEPTH, D_POOL // MXU_N, MXU_N // POOL_GROUP_W, POOL_GROUP_W, POOL_GROUP_W)))
    hpt = MXU_N // LRU_HEAD_W
    nkb = D_LRU // MXU_N
    wr = _block_diag(lru_w_r.reshape(DEPTH, 2, nkb, hpt, LRU_HEAD_W, LRU_HEAD_W))
    wi = _block_diag(lru_w_i.reshape(DEPTH, 2, nkb, hpt, LRU_HEAD_W, LRU_HEAD_W))
    wgt = bf(jnp.concatenate([wr, wi], axis=-1))
    bgt = jnp.concatenate([lru_b_r.reshape(DEPTH, 2, nkb, 1, MXU_N),
                           lru_b_i.reshape(DEPTH, 2, nkb, 1, MXU_N)], axis=-1)
    lam = lru_lambda.reshape(DEPTH, 2, 1, D_LRU)
    norm_g3 = norm_g.reshape(DEPTH * 3, 1, D_MODEL)
    conv_b3 = conv_b.reshape(DEPTH, 1, D_LRU)
    ps3 = pool_scale.reshape(DEPTH, 1, D_POOL)
    fg = final_g.reshape(1, D_MODEL)

    streams = {
        'ctx': dict(x=jnp.transpose(x_prompt, (1, 0, 2)).reshape(l_ctx * b_ctx, D_MODEL), mod=mod_ctx, bsz=b_ctx,
                    tl=TOKEN_TILE // b_ctx, pool_halo=True, icnt=_inv_counts(l_ctx, b_ctx)),
        'lat': dict(x=jnp.transpose(x_sample, (1, 0, 2)).reshape(l_lat * b_lat, D_MODEL), mod=mod_lat, bsz=b_lat,
                    tl=GRID_W, pool_halo=False, icnt=_inv_counts(GRID_W, b_lat)),
    }
    new_states = []
    for l in range(DEPTH):
        for name, s in streams.items():
            bsz = s['bsz']
            h0 = (jnp.zeros((2, bsz, D_LRU), F32) if name == 'ctx'
                  else jnp.transpose(state_lru[:, l], (1, 0, 2)))
            x = _ffn_call(s['x'], s['mod'], norm_g3, f1g, f1u, f1d, None, layer=l, sub=0, bsz=bsz)
            up, ur, gg = _inproj_call(x, s['mod'], norm_g3, w_in_b, layer=l, bsz=bsz)
            pm, yf, yb, st = _seq_call(up, ur, s['icnt'], conv_w, conv_b3, wp, ps3, wgt, bgt, lam, h0,
                                       layer=l, bsz=bsz, tl=s['tl'], pool_halo=s['pool_halo'])
            x = _outproj_call(x, s['mod'], pm, gg, yf, yb, w_out_b, layer=l, bsz=bsz)
            s['x'] = _ffn_call(x, s['mod'], norm_g3, f2g, f2u, f2d, fg if l == DEPTH - 1 else None,
                               layer=l, sub=2, bsz=bsz)
            if name == 'ctx':
                new_states.append(st)
    y_prompt = jnp.transpose(streams['ctx']['x'].reshape(l_ctx, b_ctx, D_MODEL), (1, 0, 2))
    y_sample = jnp.transpose(streams['lat']['x'].reshape(l_lat, b_lat, D_MODEL), (1, 0, 2))
    new_state_lru = jnp.transpose(jnp.stack(new_states, axis=0), (2, 0, 1, 3))
    return (y_prompt, y_sample, new_state_lru)
```

```python
import functools

import numpy as np
import jax
import jax.numpy as jnp
from jax import lax
from jax.experimental import pallas as pl
from jax.experimental.pallas import tpu as pltpu

D_MODEL = 1024
DEPTH = 4
GRID_W = 64
D_POOL = 512
N_POOL_GROUPS = 4
POOL_GROUP_W = D_POOL // N_POOL_GROUPS
POOL_WINDOWS = (2, 4, 8, 16)
POOL_HALO = max(POOL_WINDOWS) // 2
D_LRU = 512
N_LRU_HEADS = 8
LRU_HEAD_W = D_LRU // N_LRU_HEADS
CONV_W = 4
CONV_BACK = CONV_W // 2
CONV_FWD = CONV_W - 1 - CONV_BACK
LRU_C = 8.0
D_MIX = D_POOL + D_LRU
D_IN = D_POOL + 2 * D_LRU
D_FF = 2816
N_MOD = 9
EPS = 1e-6

MXU_N = 256
FF_CHUNK = MXU_N
N_FF_CHUNKS = D_FF // FF_CHUNK
N_LRU_TILES = D_LRU // MXU_N
MOD_ROWS = 16
ADA_NB = 1152
TOKEN_TILE = 512
FFN_PIECE = 64
VMEM_LIMIT = 56 * 1024 * 1024

BF16 = jnp.bfloat16
F32 = jnp.float32


def _dot(a, b):
    return jnp.dot(a, b, preferred_element_type=F32)


def _rms_norm(x, g):
    ms = jnp.mean(x * x, axis=-1, keepdims=True)
    return x * lax.rsqrt(ms + EPS) * g


def _per_batch(x, bsz, fn):
    n, d = x.shape
    return fn(x.reshape(n // bsz, bsz, d)).reshape(n, d)


def _modulated(x, g, shift, scale, bsz):
    return _per_batch(_rms_norm(x, g), bsz, lambda v: v * (1.0 + scale)[None] + shift[None]).astype(BF16)


def _params(sem):
    return pltpu.CompilerParams(dimension_semantics=sem, vmem_limit_bytes=VMEM_LIMIT)


def _ada_body(c_ref, w_ref, b_ref, o_ref):
    c = c_ref[...]
    s = (c * jax.nn.sigmoid(c)).astype(BF16)
    o_ref[...] = _dot(s, w_ref[...].astype(BF16)) + b_ref[...]


def _ada_call(cmat, w_ada, b_ada):
    nb = (N_MOD * D_MODEL) // ADA_NB
    return pl.pallas_call(
        _ada_body,
        grid=(DEPTH, nb),
        in_specs=[
            pl.BlockSpec((MOD_ROWS, D_MODEL), lambda l, j: (0, 0)),
            pl.BlockSpec((None, D_MODEL, ADA_NB), lambda l, j: (l, 0, j)),
            pl.BlockSpec((None, 1, ADA_NB), lambda l, j: (l, 0, j)),
        ],
        out_specs=pl.BlockSpec((None, MOD_ROWS, ADA_NB), lambda l, j: (l, 0, j)),
        out_shape=jax.ShapeDtypeStruct((DEPTH, MOD_ROWS, N_MOD * D_MODEL), F32),
        compiler_params=_params(("arbitrary", "arbitrary")),
        name="adaln",
    )(cmat, w_ada, b_ada.reshape(DEPTH, 1, N_MOD * D_MODEL))


def _ffn_body(bsz, final, x_ref, xnext_ref, mod_ref, g_ref, wg_ref, wu_ref, wd_ref, *rest):
    if final:
        fg_ref, o_ref, hb_ref = rest
    else:
        o_ref, hb_ref = rest
    i = pl.program_id(0)
    slot = i % 2
    shift, scale, gate = mod_ref[0], mod_ref[1], mod_ref[2]
    n_pieces = TOKEN_TILE // FFN_PIECE

    def prep(src_ref, dst_slot, p):
        rows = slice(p * FFN_PIECE, (p + 1) * FFN_PIECE)
        hb_ref[dst_slot, rows] = _modulated(src_ref[rows], g_ref[...], shift, scale, bsz)

    def gated(v):
        return _per_batch(v, bsz, lambda t: t * (0.5 * gate)[None])

    @pl.when(i == 0)
    def _():
        for p in range(n_pieces):
            prep(x_ref, 0, p)

    for c in range(N_FF_CHUNKS):
        sl = slice(c * FF_CHUNK, (c + 1) * FF_CHUNK)
        g = _dot(hb_ref[slot], wg_ref[:, sl])
        u = _dot(hb_ref[slot], wu_ref[:, sl])
        a = ((g * jax.nn.sigmoid(g)) * u).astype(BF16)
        part = gated(_dot(a, wd_ref[sl, :]))
        if c == 0:
            o_ref[...] = x_ref[...] + part
        else:
            o_ref[...] += part
        if c < n_pieces:
            prep(xnext_ref, 1 - slot, c)
    if final:
        o_ref[...] = _rms_norm(o_ref[...], fg_ref[...])


def _ffn_call(x, mod, norm_g, wg, wu, wd, final_g, *, layer, sub, bsz):
    n = x.shape[0]
    tm = TOKEN_TILE
    steps = n // tm
    final = final_g is not None
    in_specs = [
        pl.BlockSpec((tm, D_MODEL), lambda i: (i, 0)),
        pl.BlockSpec((tm, D_MODEL), lambda i: (jnp.minimum(i + 1, steps - 1), 0)),
        pl.BlockSpec((None, None, 3, bsz, D_MODEL), lambda i: (layer, sub, 0, 0, 0)),
        pl.BlockSpec((None, 1, D_MODEL), lambda i: (3 * layer + sub, 0, 0)),
        pl.BlockSpec((None, D_MODEL, D_FF), lambda i: (layer, 0, 0)),
        pl.BlockSpec((None, D_MODEL, D_FF), lambda i: (layer, 0, 0)),
        pl.BlockSpec((None, D_FF, D_MODEL), lambda i: (layer, 0, 0)),
    ]
    args = [x, x, mod, norm_g, wg, wu, wd]
    if final:
        in_specs.append(pl.BlockSpec((1, D_MODEL), lambda i: (0, 0)))
        args.append(final_g)
    return pl.pallas_call(
        functools.partial(_ffn_body, bsz, final),
        grid=(steps,),
        in_specs=in_specs,
        out_specs=pl.BlockSpec((tm, D_MODEL), lambda i: (i, 0)),
        out_shape=jax.ShapeDtypeStruct((n, D_MODEL), F32),
        scratch_shapes=[pltpu.VMEM((2, tm, D_MODEL), BF16)],
        compiler_params=_params(("arbitrary",)),
        name="ffn",
    )(*args)


def _pool_sums(e, bsz, tl):
    p2 = e[0:(tl + 15) * bsz] + e[bsz:(tl + 16) * bsz]
    p4 = p2[0:(tl + 13) * bsz] + p2[2 * bsz:(tl + 15) * bsz]
    p8 = p4[0:(tl + 9) * bsz] + p4[4 * bsz:(tl + 13) * bsz]
    p16 = p8[0:tl * bsz] + p8[8 * bsz:(tl + 8) * bsz]
    n = tl * bsz
    return {2: p2[7 * bsz:7 * bsz + n], 4: p4[6 * bsz:6 * bsz + n], 8: p8[4 * bsz:4 * bsz + n], 16: p16}


def _lru_gate_tile(xc, kb, wgt_ref, bgt_ref, lam_ref):
    sl = slice(kb * MXU_N, (kb + 1) * MXU_N)
    z = _dot(xc[:, sl].astype(BF16), wgt_ref[kb]) + bgt_ref[kb]
    tr = jnp.tanh(0.5 * z[:, :MXU_N])
    ti = jnp.tanh(0.5 * z[:, MXU_N:])
    half_rate = (0.5 * LRU_C) * jax.nn.log_sigmoid(lam_ref[:, sl])
    log_a = half_rate + half_rate * tr
    a = jnp.exp(log_a)
    y = jnp.tanh(log_a) * (-1.0 - a * a)
    b = jnp.exp2(0.5 * jnp.log2(y)) * ((0.5 + 0.5 * ti) * xc[:, sl])
    return a, b


def _lru_gates(xc, wgt_ref, bgt_ref, lam_ref):
    return [_lru_gate_tile(xc, kb, wgt_ref, bgt_ref, lam_ref) for kb in range(N_LRU_TILES)]


def _mixfwd_body(bsz, tl, pool_halo, x_ref, mod_ref, g_ref, win_ref, icnt_ref, cw_ref, cb_ref, wp_ref, ps_ref,
                 wgt_ref, bgt_ref, lam_ref, h0_ref, pm_ref, gg_ref, xc_ref, yf_ref, st_ref,
                 bufp_ref, bufr_ref, h_ref, hb_ref):
    i = pl.program_id(0)
    nt = pl.num_programs(0) - 1
    n = tl * bsz
    ph, cbk, cfw = POOL_HALO * bsz, CONV_BACK * bsz, CONV_FWD * bsz

    @pl.when(i == 0)
    def _():
        bufp_ref[...] = jnp.zeros(bufp_ref.shape, F32)
        bufr_ref[...] = jnp.zeros(bufr_ref.shape, F32)
        h_ref[...] = h0_ref[...]

    hb_ref[...] = _modulated(x_ref[...], g_ref[...], mod_ref[0], mod_ref[1], bsz)
    ur = _dot(hb_ref[...], win_ref[:, D_POOL:D_POOL + D_LRU])
    up = _dot(hb_ref[...], win_ref[:, :D_POOL])
    has_next = i < nt
    next_r = jnp.where(has_next, ur[0:cfw], 0.0)
    if pool_halo:
        next_p = jnp.where(has_next, up[0:ph], 0.0)
    else:
        next_p = jnp.zeros((ph, D_POOL), F32)

    pooled = []
    for g, w in enumerate(POOL_WINDOWS):
        sl = slice(g * POOL_GROUP_W, (g + 1) * POOL_GROUP_W)
        e = jnp.concatenate([bufp_ref[:, sl], next_p[:, sl]], axis=0)
        s = _pool_sums(e, bsz, tl)[w]
        pooled.append(s * icnt_ref[:, sl] - e[ph:ph + n])
    pooled = jnp.concatenate(pooled, axis=-1).astype(BF16)
    mixed = [_dot(pooled[:, k * MXU_N:(k + 1) * MXU_N], wp_ref[k]) for k in range(D_POOL // MXU_N)]
    pm_ref[...] = (jnp.concatenate(mixed, axis=-1) * ps_ref[...]).astype(BF16)

    ext = jnp.concatenate([bufr_ref[...], next_r], axis=0)
    xc = cb_ref[...] + ext[0:n] * cw_ref[0:1]
    for k in range(1, CONV_W):
        xc = xc + ext[k * bsz:k * bsz + n] * cw_ref[k:k + 1]
    xc_ref[...] = xc

    ab = _lru_gates(xc, wgt_ref, bgt_ref, lam_ref)
    hs = [jnp.where(i <= 1, h0_ref[:, kb * MXU_N:(kb + 1) * MXU_N], h_ref[:, kb * MXU_N:(kb + 1) * MXU_N])
          for kb in range(N_LRU_TILES)]
    for t in range(tl):
        rows = slice(t * bsz, (t + 1) * bsz)
        for kb, (a, b) in enumerate(ab):
            hs[kb] = a[rows] * hs[kb] + b[rows]
            yf_ref[rows, kb * MXU_N:(kb + 1) * MXU_N] = hs[kb]
    hlast = jnp.concatenate(hs, axis=-1)
    h_ref[...] = hlast
    st_ref[...] = hlast

    ug = _dot(hb_ref[...], win_ref[:, D_POOL + D_LRU:])
    gg_ref[...] = jax.nn.gelu(ug, approximate=True).astype(BF16)

    if pool_halo:
        bufp_ref[0:ph] = bufp_ref[n:n + ph]
    bufp_ref[ph:ph + n] = up
    bufr_ref[0:cbk] = bufr_ref[n:n + cbk]
    bufr_ref[cbk:cbk + n] = ur


def _mixfwd_call(x, mod, norm_g, w_in, icnt, conv_w, conv_b, wp, ps, wgt, bgt, lam, h0, *, layer, bsz, tl, pool_halo):
    n_rows = x.shape[0]
    n = tl * bsz
    nt = n_rows // n
    cur = lambda i: (jnp.minimum(i, nt - 1), 0)
    prev = lambda i: (jnp.maximum(i - 1, 0), 0)
    return pl.pallas_call(
        functools.partial(_mixfwd_body, bsz, tl, pool_halo),
        grid=(nt + 1,),
        in_specs=[
            pl.BlockSpec((n, D_MODEL), cur),
            pl.BlockSpec((None, None, 3, bsz, D_MODEL), lambda i: (layer, 1, 0, 0, 0)),
            pl.BlockSpec((None, 1, D_MODEL), lambda i: (3 * layer + 1, 0, 0)),
            pl.BlockSpec((None, D_MODEL, D_IN), lambda i: (layer, 0, 0)),
            pl.BlockSpec((n, D_POOL), prev if pool_halo else (lambda i: (0, 0))),
            pl.BlockSpec((None, CONV_W, D_LRU), lambda i: (layer, 0, 0)),
            pl.BlockSpec((None, 1, D_LRU), lambda i: (layer, 0, 0)),
            pl.BlockSpec((None, D_POOL // MXU_N, MXU_N, MXU_N), lambda i: (layer, 0, 0, 0)),
            pl.BlockSpec((None, 1, D_POOL), lambda i: (layer, 0, 0)),
            pl.BlockSpec((None, None, N_LRU_TILES, MXU_N, 2 * MXU_N), lambda i: (layer, 0, 0, 0, 0)),
            pl.BlockSpec((None, None, N_LRU_TILES, 1, 2 * MXU_N), lambda i: (layer, 0, 0, 0, 0)),
            pl.BlockSpec((None, None, 1, D_LRU), lambda i: (layer, 0, 0, 0)),
            pl.BlockSpec((None, bsz, D_LRU), lambda i: (0, 0, 0)),
        ],
        out_specs=[
            pl.BlockSpec((n, D_POOL), prev),
            pl.BlockSpec((n, D_LRU), cur),
            pl.BlockSpec((n, D_LRU), prev),
            pl.BlockSpec((n, D_LRU), prev),
            pl.BlockSpec((bsz, D_LRU), lambda i: (0, 0)),
        ],
        out_shape=[
            jax.ShapeDtypeStruct((n_rows, D_POOL), BF16),
            jax.ShapeDtypeStruct((n_rows, D_LRU), BF16),
            jax.ShapeDtypeStruct((n_rows, D_LRU), F32),
            jax.ShapeDtypeStruct((n_rows, D_LRU), F32),
            jax.ShapeDtypeStruct((bsz, D_LRU), F32),
        ],
        scratch_shapes=[
            pltpu.VMEM(((POOL_HALO + tl) * bsz, D_POOL), F32),
            pltpu.VMEM(((CONV_BACK + tl) * bsz, D_LRU), F32),
            pltpu.VMEM((bsz, D_LRU), F32),
            pltpu.VMEM((n, D_MODEL), BF16),
        ],
        compiler_params=_params(("arbitrary",)),
        name="mixfwd",
    )(x, mod, norm_g, w_in, icnt, conv_w, conv_b, wp, ps, wgt, bgt, lam, h0)


def _mixbwd_body(bsz, tl, x_ref, mod_ref, pm_ref, gg_ref, xc_ref, yf_ref, wgt_ref, bgt_ref, lam_ref, h0_ref, wout_ref,
                 o_ref, st_ref, rec_ref, lhs_ref, yb_ref, h_ref):
    i = pl.program_id(0)
    nt = pl.num_programs(0) - 1
    gate = mod_ref[0]

    @pl.when(i == 0)
    def _():
        rec_ref[...] = jnp.zeros(rec_ref.shape, BF16)
        h_ref[...] = h0_ref[...]

    lhs_ref[:, :D_POOL] = pm_ref[...]
    lhs_ref[:, D_POOL:] = rec_ref[...]

    ab = _lru_gates(xc_ref[...], wgt_ref, bgt_ref, lam_ref)
    hs = [h_ref[:, kb * MXU_N:(kb + 1) * MXU_N] for kb in range(N_LRU_TILES)]
    for t in reversed(range(tl)):
        rows = slice(t * bsz, (t + 1) * bsz)
        for kb, (a, b) in enumerate(ab):
            sl = slice(kb * MXU_N, (kb + 1) * MXU_N)
            hs[kb] = a[rows] * hs[kb] + b[rows]
            yb_ref[rows, sl] = hs[kb]
    rec_ref[...] = ((yf_ref[...] + yb_ref[...]) * gg_ref[...].astype(F32)).astype(BF16)
    hlast = jnp.concatenate(hs, axis=-1)
    h_ref[...] = hlast

    y = _dot(lhs_ref[...], wout_ref[...])
    o_ref[...] = x_ref[...] + _per_batch(y, bsz, lambda v: v * gate[None])

    @pl.when(i == nt - 1)
    def _():
        st_ref[...] = hlast


def _mixbwd_call(x, mod, pm, gg, xc, yf, wgt, bgt, lam, h0, w_out, *, layer, bsz, tl):
    n_rows = x.shape[0]
    n = tl * bsz
    nt = n_rows // n
    cur = lambda i: (jnp.maximum(nt - 1 - i, 0), 0)
    prev = lambda i: (jnp.minimum(nt - i, nt - 1), 0)
    return pl.pallas_call(
        functools.partial(_mixbwd_body, bsz, tl),
        grid=(nt + 1,),
        in_specs=[
            pl.BlockSpec((n, D_MODEL), prev),
            pl.BlockSpec((None, None, 1, bsz, D_MODEL), lambda i: (layer, 1, 2, 0, 0)),
            pl.BlockSpec((n, D_POOL), prev),
            pl.BlockSpec((n, D_LRU), cur),
            pl.BlockSpec((n, D_LRU), cur),
            pl.BlockSpec((n, D_LRU), cur),
            pl.BlockSpec((None, None, N_LRU_TILES, MXU_N, 2 * MXU_N), lambda i: (layer, 1, 0, 0, 0)),
            pl.BlockSpec((None, None, N_LRU_TILES, 1, 2 * MXU_N), lambda i: (layer, 1, 0, 0, 0)),
            pl.BlockSpec((None, None, 1, D_LRU), lambda i: (layer, 1, 0, 0)),
            pl.BlockSpec((None, bsz, D_LRU), lambda i: (1, 0, 0)),
            pl.BlockSpec((None, D_MIX, D_MODEL), lambda i: (layer, 0, 0)),
        ],
        out_specs=[
            pl.BlockSpec((n, D_MODEL), prev),
            pl.BlockSpec((bsz, D_LRU), lambda i: (0, 0)),
        ],
        out_shape=[
            jax.ShapeDtypeStruct((n_rows, D_MODEL), F32),
            jax.ShapeDtypeStruct((bsz, D_LRU), F32),
        ],
        scratch_shapes=[
            pltpu.VMEM((n, D_LRU), BF16),
            pltpu.VMEM((n, D_MIX), BF16),
            pltpu.VMEM((n, D_LRU), F32),
            pltpu.VMEM((bsz, D_LRU), F32),
        ],
        compiler_params=_params(("arbitrary",)),
        name="mixbwd",
    )(x, mod, pm, gg, xc, yf, wgt, bgt, lam, h0, w_out)


def _inv_counts(n_pos, bsz):
    t = np.arange(n_pos)
    cols = []
    for w in POOL_WINDOWS:
        cnt = np.clip(t + w // 2, 0, n_pos) - np.clip(t - w // 2, 0, n_pos)
        cols.append(np.repeat((1.0 / cnt)[:, None], POOL_GROUP_W, axis=1))
    tab = np.concatenate(cols, axis=1).astype(np.float32)
    return jnp.asarray(np.repeat(tab, bsz, axis=0))


def _block_diag(w):
    heads, hw = w.shape[-3], w.shape[-1]
    eye = jnp.eye(heads, dtype=w.dtype)
    full = jnp.einsum('...hij,hg->...higj', w, eye)
    return full.reshape(w.shape[:-3] + (heads * hw, heads * hw))


def kernel(x_prompt, x_sample, state_lru, c, c_ctx, norm_g, w_ada, b_ada, ffn1_gate, ffn1_up, ffn1_down,
           w_in, conv_w, conv_b, w_pool, pool_scale, lru_w_r, lru_b_r, lru_w_i, lru_b_i, lru_lambda,
           w_out, ffn2_gate, ffn2_up, ffn2_down, final_g):
    b_ctx, l_ctx, _ = x_prompt.shape
    b_lat, l_lat, _ = x_sample.shape

    cmat = jnp.zeros((MOD_ROWS, D_MODEL), F32).at[:b_lat].set(c).at[b_lat].set(c_ctx)
    mod = _ada_call(cmat, w_ada, b_ada).reshape(DEPTH, MOD_ROWS, N_MOD, D_MODEL)
    mod_lat = jnp.transpose(mod[:, :b_lat], (0, 2, 1, 3)).reshape(DEPTH, 3, 3, b_lat, D_MODEL)
    mod_ctx = jnp.broadcast_to(mod[:, b_lat][:, :, None, :], (DEPTH, N_MOD, b_ctx, D_MODEL))
    mod_ctx = mod_ctx.reshape(DEPTH, 3, 3, b_ctx, D_MODEL)

    bf = lambda w: w.astype(BF16)
    f1g, f1u, f1d, f2g, f2u, f2d = (bf(w) for w in (ffn1_gate, ffn1_up, ffn1_down, ffn2_gate, ffn2_up, ffn2_down))
    w_in_b, w_out_b = bf(w_in), bf(w_out)
    wp = bf(_block_diag(w_pool.reshape(DEPTH, D_POOL // MXU_N, MXU_N // POOL_GROUP_W, POOL_GROUP_W, POOL_GROUP_W)))
    hpt = MXU_N // LRU_HEAD_W
    wr = _block_diag(lru_w_r.reshape(DEPTH, 2, N_LRU_TILES, hpt, LRU_HEAD_W, LRU_HEAD_W))
    wi = _block_diag(lru_w_i.reshape(DEPTH, 2, N_LRU_TILES, hpt, LRU_HEAD_W, LRU_HEAD_W))
    wgt = bf(jnp.concatenate([wr, wi], axis=-1))
    bgt = jnp.concatenate([lru_b_r.reshape(DEPTH, 2, N_LRU_TILES, 1, MXU_N),
                           lru_b_i.reshape(DEPTH, 2, N_LRU_TILES, 1, MXU_N)], axis=-1)
    lam = lru_lambda.reshape(DEPTH, 2, 1, D_LRU)
    norm_g3 = norm_g.reshape(DEPTH * 3, 1, D_MODEL)
    conv_b3 = conv_b.reshape(DEPTH, 1, D_LRU)
    ps3 = pool_scale.reshape(DEPTH, 1, D_POOL)
    fg = final_g.reshape(1, D_MODEL)

    streams = {
        'ctx': dict(x=jnp.transpose(x_prompt, (1, 0, 2)).reshape(l_ctx * b_ctx, D_MODEL), mod=mod_ctx, bsz=b_ctx,
                    tl=TOKEN_TILE // b_ctx, pool_halo=True, icnt=_inv_counts(l_ctx, b_ctx)),
        'lat': dict(x=jnp.transpose(x_sample, (1, 0, 2)).reshape(l_lat * b_lat, D_MODEL), mod=mod_lat, bsz=b_lat,
                    tl=GRID_W, pool_halo=False, icnt=_inv_counts(GRID_W, b_lat)),
    }
    new_states = []
    for l in range(DEPTH):
        for name, s in streams.items():
            bsz, tl = s['bsz'], s['tl']
            h0 = (jnp.zeros((2, bsz, D_LRU), F32) if name == 'ctx'
                  else jnp.transpose(state_lru[:, l], (1, 0, 2)))
            x = _ffn_call(s['x'], s['mod'], norm_g3, f1g, f1u, f1d, None, layer=l, sub=0, bsz=bsz)
            pm, gg, xc, yf, st_f = _mixfwd_call(x, s['mod'], norm_g3, w_in_b, s['icnt'], conv_w, conv_b3, wp, ps3,
                                                wgt, bgt, lam, h0, layer=l, bsz=bsz, tl=tl, pool_halo=s['pool_halo'])
            x, st_b = _mixbwd_call(x, s['mod'], pm, gg, xc, yf, wgt, bgt, lam, h0, w_out_b, layer=l, bsz=bsz, tl=tl)
            s['x'] = _ffn_call(x, s['mod'], norm_g3, f2g, f2u, f2d, fg if l == DEPTH - 1 else None,
                               layer=l, sub=2, bsz=bsz)
            if name == 'ctx':
                new_states.append(jnp.stack([st_f, st_b], axis=0))
    y_prompt = jnp.transpose(streams['ctx']['x'].reshape(l_ctx, b_ctx, D_MODEL), (1, 0, 2))
    y_sample = jnp.transpose(streams['lat']['x'].reshape(l_lat, b_lat, D_MODEL), (1, 0, 2))
    new_state_lru = jnp.transpose(jnp.stack(new_states, axis=0), (2, 0, 1, 3))
    return (y_prompt, y_sample, new_state_lru)
```

```python
import functools

import numpy as np
import jax
import jax.numpy as jnp
from jax import lax
from jax.experimental import pallas as pl
from jax.experimental.pallas import tpu as pltpu

D_MODEL = 1024
DEPTH = 4
GRID_W = 64
D_POOL = 512
N_POOL_GROUPS = 4
POOL_GROUP_W = D_POOL // N_POOL_GROUPS
POOL_WINDOWS = (2, 4, 8, 16)
POOL_HALO = max(POOL_WINDOWS) // 2
D_LRU = 512
N_LRU_HEADS = 8
LRU_HEAD_W = D_LRU // N_LRU_HEADS
CONV_W = 4
CONV_BACK = CONV_W // 2
CONV_FWD = CONV_W - 1 - CONV_BACK
LRU_C = 8.0
D_MIX = D_POOL + D_LRU
D_IN = D_POOL + 2 * D_LRU
D_FF = 2816
N_MOD = 9
EPS = 1e-6

MXU_N = 256
FF_CHUNK = MXU_N
N_FF_CHUNKS = D_FF // FF_CHUNK
N_LRU_TILES = D_LRU // MXU_N
MOD_ROWS = 16
ADA_NB = 1152
TOKEN_TILE = 512
FFN_TILE = 1024
FFN_PIECE = 128
VMEM_LIMIT = 56 * 1024 * 1024

BF16 = jnp.bfloat16
F32 = jnp.float32


def _dot(a, b):
    return jnp.dot(a, b, preferred_element_type=F32)


def _rms_norm(x, g):
    ms = jnp.mean(x * x, axis=-1, keepdims=True)
    return x * lax.rsqrt(ms + EPS) * g


def _per_batch(x, bsz, fn):
    n, d = x.shape
    return fn(x.reshape(n // bsz, bsz, d)).reshape(n, d)


def _modulated(x, g, shift, scale, bsz):
    return _per_batch(_rms_norm(x, g), bsz, lambda v: v * (1.0 + scale)[None] + shift[None]).astype(BF16)


def _params(sem):
    return pltpu.CompilerParams(dimension_semantics=sem, vmem_limit_bytes=VMEM_LIMIT)


def _ada_body(c_ref, w_ref, b_ref, o_ref):
    c = c_ref[...]
    s = (c * jax.nn.sigmoid(c)).astype(BF16)
    o_ref[...] = _dot(s, w_ref[...].astype(BF16)) + b_ref[...]


def _ada_call(cmat, w_ada, b_ada):
    nb = (N_MOD * D_MODEL) // ADA_NB
    return pl.pallas_call(
        _ada_body,
        grid=(DEPTH, nb),
        in_specs=[
            pl.BlockSpec((MOD_ROWS, D_MODEL), lambda l, j: (0, 0)),
            pl.BlockSpec((None, D_MODEL, ADA_NB), lambda l, j: (l, 0, j)),
            pl.BlockSpec((None, 1, ADA_NB), lambda l, j: (l, 0, j)),
        ],
        out_specs=pl.BlockSpec((None, MOD_ROWS, ADA_NB), lambda l, j: (l, 0, j)),
        out_shape=jax.ShapeDtypeStruct((DEPTH, MOD_ROWS, N_MOD * D_MODEL), F32),
        compiler_params=_params(("arbitrary", "arbitrary")),
        name="adaln",
    )(cmat, w_ada, b_ada.reshape(DEPTH, 1, N_MOD * D_MODEL))


def _ffn_body(bsz, final, x_ref, xnext_ref, mod_ref, g_ref, wg_ref, wu_ref, wd_ref, *rest):
    if final:
        fg_ref, o_ref, hb_ref = rest
    else:
        o_ref, hb_ref = rest
    i = pl.program_id(0)
    slot = i % 2
    shift, scale, gate = mod_ref[0], mod_ref[1], mod_ref[2]
    n_pieces = FFN_TILE // FFN_PIECE

    def prep(src_ref, dst_slot, p):
        rows = slice(p * FFN_PIECE, (p + 1) * FFN_PIECE)
        hb_ref[dst_slot, rows] = _modulated(src_ref[rows], g_ref[...], shift, scale, bsz)

    def gated(v):
        return _per_batch(v, bsz, lambda t: t * (0.5 * gate)[None])

    @pl.when(i == 0)
    def _():
        for p in range(n_pieces):
            prep(x_ref, 0, p)

    def gate_up(c):
        sl = slice(c * FF_CHUNK, (c + 1) * FF_CHUNK)
        return _dot(hb_ref[slot], wg_ref[:, sl]), _dot(hb_ref[slot], wu_ref[:, sl])

    g, u = gate_up(0)
    for c in range(N_FF_CHUNKS):
        sl = slice(c * FF_CHUNK, (c + 1) * FF_CHUNK)
        a = ((g * jax.nn.sigmoid(g)) * u).astype(BF16)
        if c + 1 < N_FF_CHUNKS:
            g, u = gate_up(c + 1)
        part = gated(_dot(a, wd_ref[sl, :]))
        if c == 0:
            o_ref[...] = x_ref[...] + part
        else:
            o_ref[...] += part
        if c < n_pieces:
            prep(xnext_ref, 1 - slot, c)
    if final:
        o_ref[...] = _rms_norm(o_ref[...], fg_ref[...])


def _ffn_call(x, mod, norm_g, wg, wu, wd, final_g, *, layer, sub, bsz):
    n = x.shape[0]
    tm = FFN_TILE
    steps = n // tm
    final = final_g is not None
    in_specs = [
        pl.BlockSpec((tm, D_MODEL), lambda i: (i, 0)),
        pl.BlockSpec((tm, D_MODEL), lambda i: (jnp.minimum(i + 1, steps - 1), 0)),
        pl.BlockSpec((None, None, 3, bsz, D_MODEL), lambda i: (layer, sub, 0, 0, 0)),
        pl.BlockSpec((None, 1, D_MODEL), lambda i: (3 * layer + sub, 0, 0)),
        pl.BlockSpec((None, D_MODEL, D_FF), lambda i: (layer, 0, 0)),
        pl.BlockSpec((None, D_MODEL, D_FF), lambda i: (layer, 0, 0)),
        pl.BlockSpec((None, D_FF, D_MODEL), lambda i: (layer, 0, 0)),
    ]
    args = [x, x, mod, norm_g, wg, wu, wd]
    if final:
        in_specs.append(pl.BlockSpec((1, D_MODEL), lambda i: (0, 0)))
        args.append(final_g)
    return pl.pallas_call(
        functools.partial(_ffn_body, bsz, final),
        grid=(steps,),
        in_specs=in_specs,
        out_specs=pl.BlockSpec((tm, D_MODEL), lambda i: (i, 0)),
        out_shape=jax.ShapeDtypeStruct((n, D_MODEL), F32),
        scratch_shapes=[pltpu.VMEM((2, tm, D_MODEL), BF16)],
        compiler_params=_params(("arbitrary",)),
        name="ffn",
    )(*args)


def _pool_sums(e, bsz, tl):
    p2 = e[0:(tl + 15) * bsz] + e[bsz:(tl + 16) * bsz]
    p4 = p2[0:(tl + 13) * bsz] + p2[2 * bsz:(tl + 15) * bsz]
    p8 = p4[0:(tl + 9) * bsz] + p4[4 * bsz:(tl + 13) * bsz]
    p16 = p8[0:tl * bsz] + p8[8 * bsz:(tl + 8) * bsz]
    n = tl * bsz
    return {2: p2[7 * bsz:7 * bsz + n], 4: p4[6 * bsz:6 * bsz + n], 8: p8[4 * bsz:4 * bsz + n], 16: p16}


def _lru_gate_tile(xc, xcb, kb, wgt_ref, bgt_ref, lam_ref):
    sl = slice(kb * MXU_N, (kb + 1) * MXU_N)
    z = _dot(xcb[:, sl], wgt_ref[kb]) + bgt_ref[kb]
    tr = jnp.tanh(0.5 * z[:, :MXU_N])
    ti = jnp.tanh(0.5 * z[:, MXU_N:])
    half_rate = (0.5 * LRU_C) * jax.nn.log_sigmoid(lam_ref[:, sl])
    log_a = half_rate + half_rate * tr
    a = jnp.exp(log_a)
    y = jnp.tanh(log_a) * (-1.0 - a * a)
    b = jnp.exp2(0.5 * jnp.log2(y)) * ((0.5 + 0.5 * ti) * xc[:, sl])
    return a, b


def _lru_gates(xc, xcb, wgt_ref, bgt_ref, lam_ref):
    return [_lru_gate_tile(xc, xcb, kb, wgt_ref, bgt_ref, lam_ref) for kb in range(N_LRU_TILES)]


def _mixfwd_body(bsz, tl, pool_halo, x_ref, mod_ref, g_ref, win_ref, icnt_ref, cw_ref, cb_ref, wp_ref, ps_ref,
                 wgt_ref, bgt_ref, lam_ref, h0_ref, pm_ref, gg_ref, xc_ref, yf_ref, st_ref,
                 bufp_ref, bufr_ref, h_ref, hb_ref, ys_ref):
    i = pl.program_id(0)
    nt = pl.num_programs(0) - 1
    n = tl * bsz
    ph, cbk, cfw = POOL_HALO * bsz, CONV_BACK * bsz, CONV_FWD * bsz

    @pl.when(i == 0)
    def _():
        bufp_ref[...] = jnp.zeros(bufp_ref.shape, F32)
        bufr_ref[...] = jnp.zeros(bufr_ref.shape, F32)
        h_ref[...] = h0_ref[...]

    hb_ref[...] = _modulated(x_ref[...], g_ref[...], mod_ref[0], mod_ref[1], bsz)
    ur = _dot(hb_ref[...], win_ref[:, D_POOL:D_POOL + D_LRU])
    up = _dot(hb_ref[...], win_ref[:, :D_POOL])
    has_next = i < nt
    next_r = jnp.where(has_next, ur[0:cfw], 0.0)
    if pool_halo:
        next_p = jnp.where(has_next, up[0:ph], 0.0)
    else:
        next_p = jnp.zeros((ph, D_POOL), F32)

    pooled = []
    for g, w in enumerate(POOL_WINDOWS):
        sl = slice(g * POOL_GROUP_W, (g + 1) * POOL_GROUP_W)
        e = jnp.concatenate([bufp_ref[:, sl], next_p[:, sl]], axis=0)
        s = _pool_sums(e, bsz, tl)[w]
        pooled.append(s * icnt_ref[:, sl] - e[ph:ph + n])
    pooled = jnp.concatenate(pooled, axis=-1).astype(BF16)
    mixed = [_dot(pooled[:, k * MXU_N:(k + 1) * MXU_N], wp_ref[k]) for k in range(D_POOL // MXU_N)]
    pm_ref[...] = (jnp.concatenate(mixed, axis=-1) * ps_ref[...]).astype(BF16)

    ext = jnp.concatenate([bufr_ref[...], next_r], axis=0)
    xc = cb_ref[...] + ext[0:n] * cw_ref[0:1]
    for k in range(1, CONV_W):
        xc = xc + ext[k * bsz:k * bsz + n] * cw_ref[k:k + 1]
    xcb = xc.astype(BF16)
    xc_ref[...] = xcb

    ab = _lru_gates(xc, xcb, wgt_ref, bgt_ref, lam_ref)
    hs = [jnp.where(i <= 1, h0_ref[:, kb * MXU_N:(kb + 1) * MXU_N], h_ref[:, kb * MXU_N:(kb + 1) * MXU_N])
          for kb in range(N_LRU_TILES)]
    for t in range(tl):
        rows = slice(t * bsz, (t + 1) * bsz)
        for kb, (a, b) in enumerate(ab):
            hs[kb] = a[rows] * hs[kb] + b[rows]
            ys_ref[rows, kb * MXU_N:(kb + 1) * MXU_N] = hs[kb]
    yf_ref[...] = ys_ref[...].astype(BF16)
    hlast = jnp.concatenate(hs, axis=-1)
    h_ref[...] = hlast
    st_ref[...] = hlast

    ug = _dot(hb_ref[...], win_ref[:, D_POOL + D_LRU:])
    gg_ref[...] = jax.nn.gelu(ug, approximate=True).astype(BF16)

    if pool_halo:
        bufp_ref[0:ph] = bufp_ref[n:n + ph]
    bufp_ref[ph:ph + n] = up
    bufr_ref[0:cbk] = bufr_ref[n:n + cbk]
    bufr_ref[cbk:cbk + n] = ur


def _mixfwd_call(x, mod, norm_g, w_in, icnt, conv_w, conv_b, wp, ps, wgt, bgt, lam, h0, *, layer, bsz, tl, pool_halo):
    n_rows = x.shape[0]
    n = tl * bsz
    nt = n_rows // n
    cur = lambda i: (jnp.minimum(i, nt - 1), 0)
    prev = lambda i: (jnp.maximum(i - 1, 0), 0)
    return pl.pallas_call(
        functools.partial(_mixfwd_body, bsz, tl, pool_halo),
        grid=(nt + 1,),
        in_specs=[
            pl.BlockSpec((n, D_MODEL), cur),
            pl.BlockSpec((None, None, 3, bsz, D_MODEL), lambda i: (layer, 1, 0, 0, 0)),
            pl.BlockSpec((None, 1, D_MODEL), lambda i: (3 * layer + 1, 0, 0)),
            pl.BlockSpec((None, D_MODEL, D_IN), lambda i: (layer, 0, 0)),
            pl.BlockSpec((n, D_POOL), prev if pool_halo else (lambda i: (0, 0))),
            pl.BlockSpec((None, CONV_W, D_LRU), lambda i: (layer, 0, 0)),
            pl.BlockSpec((None, 1, D_LRU), lambda i: (layer, 0, 0)),
            pl.BlockSpec((None, D_POOL // MXU_N, MXU_N, MXU_N), lambda i: (layer, 0, 0, 0)),
            pl.BlockSpec((None, 1, D_POOL), lambda i: (layer, 0, 0)),
            pl.BlockSpec((None, None, N_LRU_TILES, MXU_N, 2 * MXU_N), lambda i: (layer, 0, 0, 0, 0)),
            pl.BlockSpec((None, None, N_LRU_TILES, 1, 2 * MXU_N), lambda i: (layer, 0, 0, 0, 0)),
            pl.BlockSpec((None, None, 1, D_LRU), lambda i: (layer, 0, 0, 0)),
            pl.BlockSpec((None, bsz, D_LRU), lambda i: (0, 0, 0)),
        ],
        out_specs=[
            pl.BlockSpec((n, D_POOL), prev),
            pl.BlockSpec((n, D_LRU), cur),
            pl.BlockSpec((n, D_LRU), prev),
            pl.BlockSpec((n, D_LRU), prev),
            pl.BlockSpec((bsz, D_LRU), lambda i: (0, 0)),
        ],
        out_shape=[
            jax.ShapeDtypeStruct((n_rows, D_POOL), BF16),
            jax.ShapeDtypeStruct((n_rows, D_LRU), BF16),
            jax.ShapeDtypeStruct((n_rows, D_LRU), BF16),
            jax.ShapeDtypeStruct((n_rows, D_LRU), BF16),
            jax.ShapeDtypeStruct((bsz, D_LRU), F32),
        ],
        scratch_shapes=[
            pltpu.VMEM(((POOL_HALO + tl) * bsz, D_POOL), F32),
            pltpu.VMEM(((CONV_BACK + tl) * bsz, D_LRU), F32),
            pltpu.VMEM((bsz, D_LRU), F32),
            pltpu.VMEM((n, D_MODEL), BF16),
            pltpu.VMEM((n, D_LRU), F32),
        ],
        compiler_params=_params(("arbitrary",)),
        name="mixfwd",
    )(x, mod, norm_g, w_in, icnt, conv_w, conv_b, wp, ps, wgt, bgt, lam, h0)


def _mixbwd_body(bsz, tl, x_ref, mod_ref, pm_ref, gg_ref, xc_ref, yf_ref, wgt_ref, bgt_ref, lam_ref, h0_ref, wout_ref,
                 o_ref, st_ref, rec_ref, lhs_ref, yb_ref, h_ref):
    i = pl.program_id(0)
    nt = pl.num_programs(0) - 1
    gate = mod_ref[0]

    @pl.when(i == 0)
    def _():
        rec_ref[...] = jnp.zeros(rec_ref.shape, BF16)
        h_ref[...] = h0_ref[...]

    lhs_ref[:, :D_POOL] = pm_ref[...]
    lhs_ref[:, D_POOL:] = rec_ref[...]

    xcb = xc_ref[...]
    ab = _lru_gates(xcb.astype(F32), xcb, wgt_ref, bgt_ref, lam_ref)
    hs = [h_ref[:, kb * MXU_N:(kb + 1) * MXU_N] for kb in range(N_LRU_TILES)]
    for t in reversed(range(tl)):
        rows = slice(t * bsz, (t + 1) * bsz)
        for kb, (a, b) in enumerate(ab):
            sl = slice(kb * MXU_N, (kb + 1) * MXU_N)
            hs[kb] = a[rows] * hs[kb] + b[rows]
            yb_ref[rows, sl] = hs[kb]
    rec_ref[...] = ((yf_ref[...].astype(F32) + yb_ref[...]) * gg_ref[...].astype(F32)).astype(BF16)
    hlast = jnp.concatenate(hs, axis=-1)
    h_ref[...] = hlast

    y = _dot(lhs_ref[...], wout_ref[...])
    o_ref[...] = x_ref[...] + _per_batch(y, bsz, lambda v: v * gate[None])

    @pl.when(i == nt - 1)
    def _():
        st_ref[...] = hlast


def _mixbwd_call(x, mod, pm, gg, xc, yf, wgt, bgt, lam, h0, w_out, *, layer, bsz, tl):
    n_rows = x.shape[0]
    n = tl * bsz
    nt = n_rows // n
    cur = lambda i: (jnp.maximum(nt - 1 - i, 0), 0)
    prev = lambda i: (jnp.minimum(nt - i, nt - 1), 0)
    return pl.pallas_call(
        functools.partial(_mixbwd_body, bsz, tl),
        grid=(nt + 1,),
        in_specs=[
            pl.BlockSpec((n, D_MODEL), prev),
            pl.BlockSpec((None, None, 1, bsz, D_MODEL), lambda i: (layer, 1, 2, 0, 0)),
            pl.BlockSpec((n, D_POOL), prev),
            pl.BlockSpec((n, D_LRU), cur),
            pl.BlockSpec((n, D_LRU), cur),
            pl.BlockSpec((n, D_LRU), cur),
            pl.BlockSpec((None, None, N_LRU_TILES, MXU_N, 2 * MXU_N), lambda i: (layer, 1, 0, 0, 0)),
            pl.BlockSpec((None, None, N_LRU_TILES, 1, 2 * MXU_N), lambda i: (layer, 1, 0, 0, 0)),
            pl.BlockSpec((None, None, 1, D_LRU), lambda i: (layer, 1, 0, 0)),
            pl.BlockSpec((None, bsz, D_LRU), lambda i: (1, 0, 0)),
            pl.BlockSpec((None, D_MIX, D_MODEL), lambda i: (layer, 0, 0)),
        ],
        out_specs=[
            pl.BlockSpec((n, D_MODEL), prev),
            pl.BlockSpec((bsz, D_LRU), lambda i: (0, 0)),
        ],
        out_shape=[
            jax.ShapeDtypeStruct((n_rows, D_MODEL), F32),
            jax.ShapeDtypeStruct((bsz, D_LRU), F32),
        ],
        scratch_shapes=[
            pltpu.VMEM((n, D_LRU), BF16),
            pltpu.VMEM((n, D_MIX), BF16),
            pltpu.VMEM((n, D_LRU), F32),
            pltpu.VMEM((bsz, D_LRU), F32),
        ],
        compiler_params=_params(("arbitrary",)),
        name="mixbwd",
    )(x, mod, pm, gg, xc, yf, wgt, bgt, lam, h0, w_out)


def _inv_counts(n_pos, bsz):
    t = np.arange(n_pos)
    cols = []
    for w in POOL_WINDOWS:
        cnt = np.clip(t + w // 2, 0, n_pos) - np.clip(t - w // 2, 0, n_pos)
        cols.append(np.repeat((1.0 / cnt)[:, None], POOL_GROUP_W, axis=1))
    tab = np.concatenate(cols, axis=1).astype(np.float32)
    return jnp.asarray(np.repeat(tab, bsz, axis=0))


def _block_diag(w):
    heads, hw = w.shape[-3], w.shape[-1]
    eye = jnp.eye(heads, dtype=w.dtype)
    full = jnp.einsum('...hij,hg->...higj', w, eye)
    return full.reshape(w.shape[:-3] + (heads * hw, heads * hw))


def kernel(x_prompt, x_sample, state_lru, c, c_ctx, norm_g, w_ada, b_ada, ffn1_gate, ffn1_up, ffn1_down,
           w_in, conv_w, conv_b, w_pool, pool_scale, lru_w_r, lru_b_r, lru_w_i, lru_b_i, lru_lambda,
           w_out, ffn2_gate, ffn2_up, ffn2_down, final_g):
    b_ctx, l_ctx, _ = x_prompt.shape
    b_lat, l_lat, _ = x_sample.shape

    cmat = jnp.zeros((MOD_ROWS, D_MODEL), F32).at[:b_lat].set(c).at[b_lat].set(c_ctx)
    mod = _ada_call(cmat, w_ada, b_ada).reshape(DEPTH, MOD_ROWS, N_MOD, D_MODEL)
    mod_lat = jnp.transpose(mod[:, :b_lat], (0, 2, 1, 3)).reshape(DEPTH, 3, 3, b_lat, D_MODEL)
    mod_ctx = jnp.broadcast_to(mod[:, b_lat][:, :, None, :], (DEPTH, N_MOD, b_ctx, D_MODEL))
    mod_ctx = mod_ctx.reshape(DEPTH, 3, 3, b_ctx, D_MODEL)

    bf = lambda w: w.astype(BF16)
    f1g, f1u, f1d, f2g, f2u, f2d = (bf(w) for w in (ffn1_gate, ffn1_up, ffn1_down, ffn2_gate, ffn2_up, ffn2_down))
    w_in_b, w_out_b = bf(w_in), bf(w_out)
    wp = bf(_block_diag(w_pool.reshape(DEPTH, D_POOL // MXU_N, MXU_N // POOL_GROUP_W, POOL_GROUP_W, POOL_GROUP_W)))
    hpt = MXU_N // LRU_HEAD_W
    wr = _block_diag(lru_w_r.reshape(DEPTH, 2, N_LRU_TILES, hpt, LRU_HEAD_W, LRU_HEAD_W))
    wi = _block_diag(lru_w_i.reshape(DEPTH, 2, N_LRU_TILES, hpt, LRU_HEAD_W, LRU_HEAD_W))
    wgt = bf(jnp.concatenate([wr, wi], axis=-1))
    bgt = jnp.concatenate([lru_b_r.reshape(DEPTH, 2, N_LRU_TILES, 1, MXU_N),
                           lru_b_i.reshape(DEPTH, 2, N_LRU_TILES, 1, MXU_N)], axis=-1)
    lam = lru_lambda.reshape(DEPTH, 2, 1, D_LRU)
    norm_g3 = norm_g.reshape(DEPTH * 3, 1, D_MODEL)
    conv_b3 = conv_b.reshape(DEPTH, 1, D_LRU)
    ps3 = pool_scale.reshape(DEPTH, 1, D_POOL)
    fg = final_g.reshape(1, D_MODEL)

    streams = {
        'ctx': dict(x=jnp.transpose(x_prompt, (1, 0, 2)).reshape(l_ctx * b_ctx, D_MODEL), mod=mod_ctx, bsz=b_ctx,
                    tl=TOKEN_TILE // b_ctx, pool_halo=True, icnt=_inv_counts(l_ctx, b_ctx)),
        'lat': dict(x=jnp.transpose(x_sample, (1, 0, 2)).reshape(l_lat * b_lat, D_MODEL), mod=mod_lat, bsz=b_lat,
                    tl=GRID_W, pool_halo=False, icnt=_inv_counts(GRID_W, b_lat)),
    }
    new_states = []
    for l in range(DEPTH):
        for name, s in streams.items():
            bsz, tl = s['bsz'], s['tl']
            h0 = (jnp.zeros((2, bsz, D_LRU), F32) if name == 'ctx'
                  else jnp.transpose(state_lru[:, l], (1, 0, 2)))
            x = _ffn_call(s['x'], s['mod'], norm_g3, f1g, f1u, f1d, None, layer=l, sub=0, bsz=bsz)
            pm, gg, xc, yf, st_f = _mixfwd_call(x, s['mod'], norm_g3, w_in_b, s['icnt'], conv_w, conv_b3, wp, ps3,
                                                wgt, bgt, lam, h0, layer=l, bsz=bsz, tl=tl, pool_halo=s['pool_halo'])
            x, st_b = _mixbwd_call(x, s['mod'], pm, gg, xc, yf, wgt, bgt, lam, h0, w_out_b, layer=l, bsz=bsz, tl=tl)
            s['x'] = _ffn_call(x, s['mod'], norm_g3, f2g, f2u, f2d, fg if l == DEPTH - 1 else None,
                               layer=l, sub=2, bsz=bsz)
            if name == 'ctx':
                new_states.append(jnp.stack([st_f, st_b], axis=0))
    y_prompt = jnp.transpose(streams['ctx']['x'].reshape(l_ctx, b_ctx, D_MODEL), (1, 0, 2))
    y_sample = jnp.transpose(streams['lat']['x'].reshape(l_lat, b_lat, D_MODEL), (1, 0, 2))
    new_state_lru = jnp.transpose(jnp.stack(new_states, axis=0), (2, 0, 1, 3))
    return (y_prompt, y_sample, new_state_lru)
```

```python
import functools

import numpy as np
import jax
import jax.numpy as jnp
from jax import lax
from jax.experimental import pallas as pl
from jax.experimental.pallas import tpu as pltpu

D_MODEL = 1024
DEPTH = 4
GRID_W = 64
D_POOL = 512
N_POOL_GROUPS = 4
POOL_GROUP_W = D_POOL // N_POOL_GROUPS
POOL_WINDOWS = (2, 4, 8, 16)
POOL_HALO = max(POOL_WINDOWS) // 2
D_LRU = 512
N_LRU_HEADS = 8
LRU_HEAD_W = D_LRU // N_LRU_HEADS
CONV_W = 4
CONV_BACK = CONV_W // 2
CONV_FWD = CONV_W - 1 - CONV_BACK
LRU_C = 8.0
D_MIX = D_POOL + D_LRU
D_IN = D_POOL + 2 * D_LRU
D_FF = 2816
N_MOD = 9
EPS = 1e-6

MXU_N = 256
FF_CHUNK = MXU_N
N_FF_CHUNKS = D_FF // FF_CHUNK
N_LRU_TILES = D_LRU // MXU_N
MOD_ROWS = 16
ADA_NB = 1152
TOKEN_TILE = 512
FFN_TILE = 1024
FFN_PIECE = 128
VMEM_LIMIT = 56 * 1024 * 1024

BF16 = jnp.bfloat16
F32 = jnp.float32


def _dot(a, b):
    return jnp.dot(a, b, preferred_element_type=F32)


def _rms_norm(x, g):
    ms = jnp.mean(x * x, axis=-1, keepdims=True)
    return x * lax.rsqrt(ms + EPS) * g


def _per_batch(x, bsz, fn):
    n, d = x.shape
    return fn(x.reshape(n // bsz, bsz, d)).reshape(n, d)


def _modulated(x, g, shift, scale, bsz):
    return _per_batch(_rms_norm(x, g), bsz, lambda v: v * (1.0 + scale)[None] + shift[None]).astype(BF16)


def _params(sem):
    return pltpu.CompilerParams(dimension_semantics=sem, vmem_limit_bytes=VMEM_LIMIT)


def _ada_body(c_ref, w_ref, b_ref, o_ref):
    c = c_ref[...]
    s = (c * jax.nn.sigmoid(c)).astype(BF16)
    o_ref[...] = _dot(s, w_ref[...].astype(BF16)) + b_ref[...]


def _ada_call(cmat, w_ada, b_ada):
    nb = (N_MOD * D_MODEL) // ADA_NB
    return pl.pallas_call(
        _ada_body,
        grid=(DEPTH, nb),
        in_specs=[
            pl.BlockSpec((MOD_ROWS, D_MODEL), lambda l, j: (0, 0)),
            pl.BlockSpec((None, D_MODEL, ADA_NB), lambda l, j: (l, 0, j)),
            pl.BlockSpec((None, 1, ADA_NB), lambda l, j: (l, 0, j)),
        ],
        out_specs=pl.BlockSpec((None, MOD_ROWS, ADA_NB), lambda l, j: (l, 0, j)),
        out_shape=jax.ShapeDtypeStruct((DEPTH, MOD_ROWS, N_MOD * D_MODEL), F32),
        compiler_params=_params(("arbitrary", "arbitrary")),
        name="adaln",
    )(cmat, w_ada, b_ada.reshape(DEPTH, 1, N_MOD * D_MODEL))


def _ffn_body(bsz, in_bt, final, xfirst_ref, xnext_ref, mod_ref, g_ref, wg_ref, wu_ref, wd_ref, *rest):
    if final:
        fg_ref, o_ref, hb_ref, xt_ref = rest
    else:
        o_ref, hb_ref, xt_ref = rest
    i = pl.program_id(0)
    slot = i % 2
    shift, scale, gate = mod_ref[0], mod_ref[1], mod_ref[2]
    n_pieces = FFN_TILE // FFN_PIECE
    pt = FFN_PIECE // bsz

    def prep(src_ref, dst_slot, p):
        rows = slice(p * FFN_PIECE, (p + 1) * FFN_PIECE)
        if in_bt:
            x = jnp.swapaxes(src_ref[:, p * pt:(p + 1) * pt, :], 0, 1).reshape(FFN_PIECE, D_MODEL)
        else:
            x = src_ref[rows]
        xt_ref[dst_slot, rows] = x
        hb_ref[dst_slot, rows] = _modulated(x, g_ref[...], shift, scale, bsz)

    def gated(v):
        return _per_batch(v, bsz, lambda t: t * (0.5 * gate)[None])

    @pl.when(i == 0)
    def _():
        for p in range(n_pieces):
            prep(xfirst_ref, 0, p)

    def gate_up(c):
        sl = slice(c * FF_CHUNK, (c + 1) * FF_CHUNK)
        return _dot(hb_ref[slot], wg_ref[:, sl]), _dot(hb_ref[slot], wu_ref[:, sl])

    g, u = gate_up(0)
    for c in range(N_FF_CHUNKS):
        sl = slice(c * FF_CHUNK, (c + 1) * FF_CHUNK)
        a = ((g * jax.nn.sigmoid(g)) * u).astype(BF16)
        if c + 1 < N_FF_CHUNKS:
            g, u = gate_up(c + 1)
        part = gated(_dot(a, wd_ref[sl, :]))
        if final:
            xt_ref[slot] = xt_ref[slot] + part
        elif c == 0:
            o_ref[...] = xt_ref[slot] + part
        else:
            o_ref[...] += part
        if c < n_pieces:
            prep(xnext_ref, 1 - slot, c)
    if final:
        for p in range(n_pieces):
            rows = slice(p * FFN_PIECE, (p + 1) * FFN_PIECE)
            y = _rms_norm(xt_ref[slot, rows], fg_ref[...])
            o_ref[:, p * pt:(p + 1) * pt, :] = jnp.swapaxes(y.reshape(pt, bsz, D_MODEL), 0, 1)


def _ffn_call(x, mod, norm_g, wg, wu, wd, final_g, *, layer, sub, bsz):
    tm = FFN_TILE
    in_bt = x.ndim == 3
    n = x.shape[0] * x.shape[1] if in_bt else x.shape[0]
    steps = n // tm
    final = final_g is not None
    bt_block = (bsz, tm // bsz, D_MODEL)
    if in_bt:
        x_specs = [pl.BlockSpec(bt_block, lambda i: (0, 0, 0)),
                   pl.BlockSpec(bt_block, lambda i: (0, jnp.minimum(i + 1, steps - 1), 0))]
    else:
        x_specs = [pl.BlockSpec((tm, D_MODEL), lambda i: (0, 0)),
                   pl.BlockSpec((tm, D_MODEL), lambda i: (jnp.minimum(i + 1, steps - 1), 0))]
    in_specs = x_specs + [
        pl.BlockSpec((None, None, 3, bsz, D_MODEL), lambda i: (layer, sub, 0, 0, 0)),
        pl.BlockSpec((None, 1, D_MODEL), lambda i: (3 * layer + sub, 0, 0)),
        pl.BlockSpec((None, D_MODEL, D_FF), lambda i: (layer, 0, 0)),
        pl.BlockSpec((None, D_MODEL, D_FF), lambda i: (layer, 0, 0)),
        pl.BlockSpec((None, D_FF, D_MODEL), lambda i: (layer, 0, 0)),
    ]
    args = [x, x, mod, norm_g, wg, wu, wd]
    if final:
        in_specs.append(pl.BlockSpec((1, D_MODEL), lambda i: (0, 0)))
        args.append(final_g)
        out_spec = pl.BlockSpec(bt_block, lambda i: (0, i, 0))
        out_shape = jax.ShapeDtypeStruct((bsz, n // bsz, D_MODEL), F32)
    else:
        out_spec = pl.BlockSpec((tm, D_MODEL), lambda i: (i, 0))
        out_shape = jax.ShapeDtypeStruct((n, D_MODEL), F32)
    return pl.pallas_call(
        functools.partial(_ffn_body, bsz, in_bt, final),
        grid=(steps,),
        in_specs=in_specs,
        out_specs=out_spec,
        out_shape=out_shape,
        scratch_shapes=[pltpu.VMEM((2, tm, D_MODEL), BF16), pltpu.VMEM((2, tm, D_MODEL), F32)],
        compiler_params=_params(("arbitrary",)),
        name="ffn",
    )(*args)


def _pool_sums(e, bsz, tl):
    p2 = e[0:(tl + 15) * bsz] + e[bsz:(tl + 16) * bsz]
    p4 = p2[0:(tl + 13) * bsz] + p2[2 * bsz:(tl + 15) * bsz]
    p8 = p4[0:(tl + 9) * bsz] + p4[4 * bsz:(tl + 13) * bsz]
    p16 = p8[0:tl * bsz] + p8[8 * bsz:(tl + 8) * bsz]
    n = tl * bsz
    return {2: p2[7 * bsz:7 * bsz + n], 4: p4[6 * bsz:6 * bsz + n], 8: p8[4 * bsz:4 * bsz + n], 16: p16}


def _lru_gate_tile(xc, xcb, kb, wgt_ref, bgt_ref, lam_ref):
    sl = slice(kb * MXU_N, (kb + 1) * MXU_N)
    z = _dot(xcb[:, sl], wgt_ref[kb]) + bgt_ref[kb]
    tr = jnp.tanh(0.5 * z[:, :MXU_N])
    ti = jnp.tanh(0.5 * z[:, MXU_N:])
    half_rate = (0.5 * LRU_C) * jax.nn.log_sigmoid(lam_ref[:, sl])
    log_a = half_rate + half_rate * tr
    a = jnp.exp(log_a)
    y = jnp.tanh(log_a) * (-1.0 - a * a)
    b = jnp.exp2(0.5 * jnp.log2(y)) * ((0.5 + 0.5 * ti) * xc[:, sl])
    return a, b


def _lru_gates(xc, xcb, wgt_ref, bgt_ref, lam_ref):
    return [_lru_gate_tile(xc, xcb, kb, wgt_ref, bgt_ref, lam_ref) for kb in range(N_LRU_TILES)]


def _mixfwd_body(bsz, tl, pool_halo, x_ref, mod_ref, g_ref, win_ref, icnt_ref, cw_ref, cb_ref, wp_ref, ps_ref,
                 wgt_ref, bgt_ref, lam_ref, h0_ref, pm_ref, gg_ref, xc_ref, yf_ref, st_ref,
                 bufp_ref, bufr_ref, h_ref, hb_ref, ys_ref):
    i = pl.program_id(0)
    nt = pl.num_programs(0) - 1
    n = tl * bsz
    ph, cbk, cfw = POOL_HALO * bsz, CONV_BACK * bsz, CONV_FWD * bsz

    @pl.when(i == 0)
    def _():
        bufp_ref[...] = jnp.zeros(bufp_ref.shape, F32)
        bufr_ref[...] = jnp.zeros(bufr_ref.shape, F32)
        h_ref[...] = h0_ref[...]

    hb_ref[...] = _modulated(x_ref[...], g_ref[...], mod_ref[0], mod_ref[1], bsz)
    ur = _dot(hb_ref[...], win_ref[:, D_POOL:D_POOL + D_LRU])
    up = _dot(hb_ref[...], win_ref[:, :D_POOL])
    has_next = i < nt
    next_r = jnp.where(has_next, ur[0:cfw], 0.0)
    if pool_halo:
        next_p = jnp.where(has_next, up[0:ph], 0.0)
    else:
        next_p = jnp.zeros((ph, D_POOL), F32)

    pooled = []
    for g, w in enumerate(POOL_WINDOWS):
        sl = slice(g * POOL_GROUP_W, (g + 1) * POOL_GROUP_W)
        e = jnp.concatenate([bufp_ref[:, sl], next_p[:, sl]], axis=0)
        s = _pool_sums(e, bsz, tl)[w]
        pooled.append(s * icnt_ref[:, sl] - e[ph:ph + n])
    pooled = jnp.concatenate(pooled, axis=-1).astype(BF16)
    mixed = [_dot(pooled[:, k * MXU_N:(k + 1) * MXU_N], wp_ref[k]) for k in range(D_POOL // MXU_N)]
    pm_ref[...] = (jnp.concatenate(mixed, axis=-1) * ps_ref[...]).astype(BF16)

    ext = jnp.concatenate([bufr_ref[...], next_r], axis=0)
    xc = cb_ref[...] + ext[0:n] * cw_ref[0:1]
    for k in range(1, CONV_W):
        xc = xc + ext[k * bsz:k * bsz + n] * cw_ref[k:k + 1]
    xcb = xc.astype(BF16)
    xc_ref[...] = xcb

    ab = _lru_gates(xc, xcb, wgt_ref, bgt_ref, lam_ref)
    hs = [jnp.where(i <= 1, h0_ref[:, kb * MXU_N:(kb + 1) * MXU_N], h_ref[:, kb * MXU_N:(kb + 1) * MXU_N])
          for kb in range(N_LRU_TILES)]
    for t in range(tl):
        rows = slice(t * bsz, (t + 1) * bsz)
        for kb, (a, b) in enumerate(ab):
            hs[kb] = a[rows] * hs[kb] + b[rows]
            ys_ref[rows, kb * MXU_N:(kb + 1) * MXU_N] = hs[kb]
    yf_ref[...] = ys_ref[...].astype(BF16)
    hlast = jnp.concatenate(hs, axis=-1)
    h_ref[...] = hlast
    st_ref[...] = hlast

    ug = _dot(hb_ref[...], win_ref[:, D_POOL + D_LRU:])
    gg_ref[...] = jax.nn.gelu(ug, approximate=True).astype(BF16)

    if pool_halo:
        bufp_ref[0:ph] = bufp_ref[n:n + ph]
    bufp_ref[ph:ph + n] = up
    bufr_ref[0:cbk] = bufr_ref[n:n + cbk]
    bufr_ref[cbk:cbk + n] = ur


def _mixfwd_call(x, mod, norm_g, w_in, icnt, conv_w, conv_b, wp, ps, wgt, bgt, lam, h0, *, layer, bsz, tl, pool_halo):
    n_rows = x.shape[0]
    n = tl * bsz
    nt = n_rows // n
    cur = lambda i: (jnp.minimum(i, nt - 1), 0)
    prev = lambda i: (jnp.maximum(i - 1, 0), 0)
    return pl.pallas_call(
        functools.partial(_mixfwd_body, bsz, tl, pool_halo),
        grid=(nt + 1,),
        in_specs=[
            pl.BlockSpec((n, D_MODEL), cur),
            pl.BlockSpec((None, None, 3, bsz, D_MODEL), lambda i: (layer, 1, 0, 0, 0)),
            pl.BlockSpec((None, 1, D_MODEL), lambda i: (3 * layer + 1, 0, 0)),
            pl.BlockSpec((None, D_MODEL, D_IN), lambda i: (layer, 0, 0)),
            pl.BlockSpec((n, D_POOL), prev if pool_halo else (lambda i: (0, 0))),
            pl.BlockSpec((None, CONV_W, D_LRU), lambda i: (layer, 0, 0)),
            pl.BlockSpec((None, 1, D_LRU), lambda i: (layer, 0, 0)),
            pl.BlockSpec((None, D_POOL // MXU_N, MXU_N, MXU_N), lambda i: (layer, 0, 0, 0)),
            pl.BlockSpec((None, 1, D_POOL), lambda i: (layer, 0, 0)),
            pl.BlockSpec((None, None, N_LRU_TILES, MXU_N, 2 * MXU_N), lambda i: (layer, 0, 0, 0, 0)),
            pl.BlockSpec((None, None, N_LRU_TILES, 1, 2 * MXU_N), lambda i: (layer, 0, 0, 0, 0)),
            pl.BlockSpec((None, None, 1, D_LRU), lambda i: (layer, 0, 0, 0)),
            pl.BlockSpec((None, bsz, D_LRU), lambda i: (0, 0, 0)),
        ],
        out_specs=[
            pl.BlockSpec((n, D_POOL), prev),
            pl.BlockSpec((n, D_LRU), cur),
            pl.BlockSpec((n, D_LRU), prev),
            pl.BlockSpec((n, D_LRU), prev),
            pl.BlockSpec((bsz, D_LRU), lambda i: (0, 0)),
        ],
        out_shape=[
            jax.ShapeDtypeStruct((n_rows, D_POOL), BF16),
            jax.ShapeDtypeStruct((n_rows, D_LRU), BF16),
            jax.ShapeDtypeStruct((n_rows, D_LRU), BF16),
            jax.ShapeDtypeStruct((n_rows, D_LRU), BF16),
            jax.ShapeDtypeStruct((bsz, D_LRU), F32),
        ],
        scratch_shapes=[
            pltpu.VMEM(((POOL_HALO + tl) * bsz, D_POOL), F32),
            pltpu.VMEM(((CONV_BACK + tl) * bsz, D_LRU), F32),
            pltpu.VMEM((bsz, D_LRU), F32),
            pltpu.VMEM((n, D_MODEL), BF16),
            pltpu.VMEM((n, D_LRU), F32),
        ],
        compiler_params=_params(("arbitrary",)),
        name="mixfwd",
    )(x, mod, norm_g, w_in, icnt, conv_w, conv_b, wp, ps, wgt, bgt, lam, h0)


def _mixbwd_body(bsz, tl, x_ref, mod_ref, pm_ref, gg_ref, xc_ref, yf_ref, wgt_ref, bgt_ref, lam_ref, h0_ref, wout_ref,
                 o_ref, st_ref, rec_ref, lhs_ref, yb_ref, h_ref):
    i = pl.program_id(0)
    nt = pl.num_programs(0) - 1
    gate = mod_ref[0]

    @pl.when(i == 0)
    def _():
        rec_ref[...] = jnp.zeros(rec_ref.shape, BF16)
        h_ref[...] = h0_ref[...]

    lhs_ref[:, :D_POOL] = pm_ref[...]
    lhs_ref[:, D_POOL:] = rec_ref[...]

    xcb = xc_ref[...]
    ab = _lru_gates(xcb.astype(F32), xcb, wgt_ref, bgt_ref, lam_ref)
    hs = [h_ref[:, kb * MXU_N:(kb + 1) * MXU_N] for kb in range(N_LRU_TILES)]
    for t in reversed(range(tl)):
        rows = slice(t * bsz, (t + 1) * bsz)
        for kb, (a, b) in enumerate(ab):
            sl = slice(kb * MXU_N, (kb + 1) * MXU_N)
            hs[kb] = a[rows] * hs[kb] + b[rows]
            yb_ref[rows, sl] = hs[kb]
    rec_ref[...] = ((yf_ref[...].astype(F32) + yb_ref[...]) * gg_ref[...].astype(F32)).astype(BF16)
    hlast = jnp.concatenate(hs, axis=-1)
    h_ref[...] = hlast

    y = _dot(lhs_ref[...], wout_ref[...])
    o_ref[...] = x_ref[...] + _per_batch(y, bsz, lambda v: v * gate[None])

    @pl.when(i == nt - 1)
    def _():
        st_ref[...] = hlast


def _mixbwd_call(x, mod, pm, gg, xc, yf, wgt, bgt, lam, h0, w_out, *, layer, bsz, tl):
    n_rows = x.shape[0]
    n = tl * bsz
    nt = n_rows // n
    cur = lambda i: (jnp.maximum(nt - 1 - i, 0), 0)
    prev = lambda i: (jnp.minimum(nt - i, nt - 1), 0)
    return pl.pallas_call(
        functools.partial(_mixbwd_body, bsz, tl),
        grid=(nt + 1,),
        in_specs=[
            pl.BlockSpec((n, D_MODEL), prev),
            pl.BlockSpec((None, None, 1, bsz, D_MODEL), lambda i: (layer, 1, 2, 0, 0)),
            pl.BlockSpec((n, D_POOL), prev),
            pl.BlockSpec((n, D_LRU), cur),
            pl.BlockSpec((n, D_LRU), cur),
            pl.BlockSpec((n, D_LRU), cur),
            pl.BlockSpec((None, None, N_LRU_TILES, MXU_N, 2 * MXU_N), lambda i: (layer, 1, 0, 0, 0)),
            pl.BlockSpec((None, None, N_LRU_TILES, 1, 2 * MXU_N), lambda i: (layer, 1, 0, 0, 0)),
            pl.BlockSpec((None, None, 1, D_LRU), lambda i: (layer, 1, 0, 0)),
            pl.BlockSpec((None, bsz, D_LRU), lambda i: (1, 0, 0)),
            pl.BlockSpec((None, D_MIX, D_MODEL), lambda i: (layer, 0, 0)),
        ],
        out_specs=[
            pl.BlockSpec((n, D_MODEL), prev),
            pl.BlockSpec((bsz, D_LRU), lambda i: (0, 0)),
        ],
        out_shape=[
            jax.ShapeDtypeStruct((n_rows, D_MODEL), F32),
            jax.ShapeDtypeStruct((bsz, D_LRU), F32),
        ],
        scratch_shapes=[
            pltpu.VMEM((n, D_LRU), BF16),
            pltpu.VMEM((n, D_MIX), BF16),
            pltpu.VMEM((n, D_LRU), F32),
            pltpu.VMEM((bsz, D_LRU), F32),
        ],
        compiler_params=_params(("arbitrary",)),
        name="mixbwd",
    )(x, mod, pm, gg, xc, yf, wgt, bgt, lam, h0, w_out)


def _inv_counts(n_pos, bsz):
    t = np.arange(n_pos)
    cols = []
    for w in POOL_WINDOWS:
        cnt = np.clip(t + w // 2, 0, n_pos) - np.clip(t - w // 2, 0, n_pos)
        cols.append(np.repeat((1.0 / cnt)[:, None], POOL_GROUP_W, axis=1))
    tab = np.concatenate(cols, axis=1).astype(np.float32)
    return jnp.asarray(np.repeat(tab, bsz, axis=0))


def _block_diag(w):
    heads, hw = w.shape[-3], w.shape[-1]
    eye = jnp.eye(heads, dtype=w.dtype)
    full = jnp.einsum('...hij,hg->...higj', w, eye)
    return full.reshape(w.shape[:-3] + (heads * hw, heads * hw))


def kernel(x_prompt, x_sample, state_lru, c, c_ctx, norm_g, w_ada, b_ada, ffn1_gate, ffn1_up, ffn1_down,
           w_in, conv_w, conv_b, w_pool, pool_scale, lru_w_r, lru_b_r, lru_w_i, lru_b_i, lru_lambda,
           w_out, ffn2_gate, ffn2_up, ffn2_down, final_g):
    b_ctx, l_ctx, _ = x_prompt.shape
    b_lat, l_lat, _ = x_sample.shape

    cmat = jnp.zeros((MOD_ROWS, D_MODEL), F32).at[:b_lat].set(c).at[b_lat].set(c_ctx)
    mod = _ada_call(cmat, w_ada, b_ada).reshape(DEPTH, MOD_ROWS, N_MOD, D_MODEL)
    mod_lat = jnp.transpose(mod[:, :b_lat], (0, 2, 1, 3)).reshape(DEPTH, 3, 3, b_lat, D_MODEL)
    mod_ctx = jnp.broadcast_to(mod[:, b_lat][:, :, None, :], (DEPTH, N_MOD, b_ctx, D_MODEL))
    mod_ctx = mod_ctx.reshape(DEPTH, 3, 3, b_ctx, D_MODEL)

    bf = lambda w: w.astype(BF16)
    f1g, f1u, f1d, f2g, f2u, f2d = (bf(w) for w in (ffn1_gate, ffn1_up, ffn1_down, ffn2_gate, ffn2_up, ffn2_down))
    w_in_b, w_out_b = bf(w_in), bf(w_out)
    wp = bf(_block_diag(w_pool.reshape(DEPTH, D_POOL // MXU_N, MXU_N // POOL_GROUP_W, POOL_GROUP_W, POOL_GROUP_W)))
    hpt = MXU_N // LRU_HEAD_W
    wr = _block_diag(lru_w_r.reshape(DEPTH, 2, N_LRU_TILES, hpt, LRU_HEAD_W, LRU_HEAD_W))
    wi = _block_diag(lru_w_i.reshape(DEPTH, 2, N_LRU_TILES, hpt, LRU_HEAD_W, LRU_HEAD_W))
    wgt = bf(jnp.concatenate([wr, wi], axis=-1))
    bgt = jnp.concatenate([lru_b_r.reshape(DEPTH, 2, N_LRU_TILES, 1, MXU_N),
                           lru_b_i.reshape(DEPTH, 2, N_LRU_TILES, 1, MXU_N)], axis=-1)
    lam = lru_lambda.reshape(DEPTH, 2, 1, D_LRU)
    norm_g3 = norm_g.reshape(DEPTH * 3, 1, D_MODEL)
    conv_b3 = conv_b.reshape(DEPTH, 1, D_LRU)
    ps3 = pool_scale.reshape(DEPTH, 1, D_POOL)
    fg = final_g.reshape(1, D_MODEL)

    streams = {
        'ctx': dict(x=x_prompt, mod=mod_ctx, bsz=b_ctx,
                    tl=TOKEN_TILE // b_ctx, pool_halo=True, icnt=_inv_counts(l_ctx, b_ctx)),
        'lat': dict(x=x_sample, mod=mod_lat, bsz=b_lat,
                    tl=GRID_W, pool_halo=False, icnt=_inv_counts(GRID_W, b_lat)),
    }
    new_states = []
    for l in range(DEPTH):
        for name, s in streams.items():
            bsz, tl = s['bsz'], s['tl']
            h0 = (jnp.zeros((2, bsz, D_LRU), F32) if name == 'ctx'
                  else jnp.transpose(state_lru[:, l], (1, 0, 2)))
            x = _ffn_call(s['x'], s['mod'], norm_g3, f1g, f1u, f1d, None, layer=l, sub=0, bsz=bsz)
            pm, gg, xc, yf, st_f = _mixfwd_call(x, s['mod'], norm_g3, w_in_b, s['icnt'], conv_w, conv_b3, wp, ps3,
                                                wgt, bgt, lam, h0, layer=l, bsz=bsz, tl=tl, pool_halo=s['pool_halo'])
            x, st_b = _mixbwd_call(x, s['mod'], pm, gg, xc, yf, wgt, bgt, lam, h0, w_out_b, layer=l, bsz=bsz, tl=tl)
            s['x'] = _ffn_call(x, s['mod'], norm_g3, f2g, f2u, f2d, fg if l == DEPTH - 1 else None,
                               layer=l, sub=2, bsz=bsz)
            if name == 'ctx':
                new_states.append(jnp.stack([st_f, st_b], axis=0))
    y_prompt, y_sample = streams['ctx']['x'], streams['lat']['x']
    new_state_lru = jnp.transpose(jnp.stack(new_states, axis=0), (2, 0, 1, 3))
    return (y_prompt, y_sample, new_state_lru)
```

```python
import functools

import numpy as np
import jax
import jax.numpy as jnp
from jax import lax
from jax.experimental import pallas as pl
from jax.experimental.pallas import tpu as pltpu

D_MODEL = 1024
DEPTH = 4
GRID_W = 64
D_POOL = 512
N_POOL_GROUPS = 4
POOL_GROUP_W = D_POOL // N_POOL_GROUPS
POOL_WINDOWS = (2, 4, 8, 16)
POOL_HALO = max(POOL_WINDOWS) // 2
D_LRU = 512
N_LRU_HEADS = 8
LRU_HEAD_W = D_LRU // N_LRU_HEADS
CONV_W = 4
CONV_BACK = CONV_W // 2
CONV_FWD = CONV_W - 1 - CONV_BACK
LRU_C = 8.0
D_MIX = D_POOL + D_LRU
D_IN = D_POOL + 2 * D_LRU
D_FF = 2816
N_MOD = 9
EPS = 1e-6

MXU_N = 256
FF_CHUNK = MXU_N
N_FF_CHUNKS = D_FF // FF_CHUNK
N_LRU_TILES = D_LRU // MXU_N
MOD_ROWS = 16
ADA_NB = 1152
TOKEN_TILE = 512
FFN_TILE = 1024
FFN_PIECE = 128
VMEM_LIMIT = 56 * 1024 * 1024

BF16 = jnp.bfloat16
F32 = jnp.float32


def _dot(a, b):
    return jnp.dot(a, b, preferred_element_type=F32)


def _rms_norm(x, g):
    ms = jnp.mean(x * x, axis=-1, keepdims=True)
    return x * lax.rsqrt(ms + EPS) * g


def _per_batch(x, bsz, fn):
    n, d = x.shape
    return fn(x.reshape(n // bsz, bsz, d)).reshape(n, d)


def _modulated(x, g, shift, scale, bsz):
    return _per_batch(_rms_norm(x, g), bsz, lambda v: v * (1.0 + scale)[None] + shift[None]).astype(BF16)


def _params(sem):
    return pltpu.CompilerParams(dimension_semantics=sem, vmem_limit_bytes=VMEM_LIMIT)


def _ada_body(c_ref, w_ref, b_ref, o_ref):
    c = c_ref[...]
    s = (c * jax.nn.sigmoid(c)).astype(BF16)
    o_ref[...] = _dot(s, w_ref[...].astype(BF16)) + b_ref[...]


def _ada_call(cmat, w_ada, b_ada):
    nb = (N_MOD * D_MODEL) // ADA_NB
    return pl.pallas_call(
        _ada_body,
        grid=(DEPTH, nb),
        in_specs=[
            pl.BlockSpec((MOD_ROWS, D_MODEL), lambda l, j: (0, 0)),
            pl.BlockSpec((None, D_MODEL, ADA_NB), lambda l, j: (l, 0, j)),
            pl.BlockSpec((None, 1, ADA_NB), lambda l, j: (l, 0, j)),
        ],
        out_specs=pl.BlockSpec((None, MOD_ROWS, ADA_NB), lambda l, j: (l, 0, j)),
        out_shape=jax.ShapeDtypeStruct((DEPTH, MOD_ROWS, N_MOD * D_MODEL), F32),
        compiler_params=_params(("arbitrary", "arbitrary")),
        name="adaln",
    )(cmat, w_ada, b_ada.reshape(DEPTH, 1, N_MOD * D_MODEL))


def _ffn_body(bsz, in_bt, final, x_ref, xnext_ref, mod_ref, g_ref, wg_ref, wu_ref, wd_ref, *rest):
    rest = list(rest)
    xt_ref = rest.pop() if in_bt else None
    if final:
        fg_ref, o_ref, hb_ref = rest
    else:
        o_ref, hb_ref = rest
    i = pl.program_id(0)
    slot = i % 2
    shift, scale, gate = mod_ref[0], mod_ref[1], mod_ref[2]
    n_pieces = FFN_TILE // FFN_PIECE
    pt = FFN_PIECE // bsz

    def prep(src_ref, dst_slot, p):
        rows = slice(p * FFN_PIECE, (p + 1) * FFN_PIECE)
        if in_bt:
            x = jnp.swapaxes(src_ref[:, p * pt:(p + 1) * pt, :], 0, 1).reshape(FFN_PIECE, D_MODEL)
            xt_ref[dst_slot, rows] = x
        else:
            x = src_ref[rows]
        hb_ref[dst_slot, rows] = _modulated(x, g_ref[...], shift, scale, bsz)

    def gated(v):
        return _per_batch(v, bsz, lambda t: t * (0.5 * gate)[None])

    @pl.when(i == 0)
    def _():
        for p in range(n_pieces):
            prep(x_ref, 0, p)

    def gate_up(c):
        sl = slice(c * FF_CHUNK, (c + 1) * FF_CHUNK)
        return _dot(hb_ref[slot], wg_ref[:, sl]), _dot(hb_ref[slot], wu_ref[:, sl])

    g, u = gate_up(0)
    for c in range(N_FF_CHUNKS):
        sl = slice(c * FF_CHUNK, (c + 1) * FF_CHUNK)
        a = ((g * jax.nn.sigmoid(g)) * u).astype(BF16)
        if c + 1 < N_FF_CHUNKS:
            g, u = gate_up(c + 1)
        part = gated(_dot(a, wd_ref[sl, :]))
        if c == 0:
            o_ref[...] = (xt_ref[slot] if in_bt else x_ref[...]) + part
        else:
            o_ref[...] += part
        if c < n_pieces:
            prep(xnext_ref, 1 - slot, c)
    if final:
        o_ref[...] = _rms_norm(o_ref[...], fg_ref[...])


def _ffn_call(x, mod, norm_g, wg, wu, wd, final_g, *, layer, sub, bsz):
    tm = FFN_TILE
    in_bt = x.ndim == 3
    n = x.shape[0] * x.shape[1] if in_bt else x.shape[0]
    steps = n // tm
    final = final_g is not None
    nxt = lambda i: jnp.minimum(i + 1, steps - 1)
    if in_bt:
        bt_block = (bsz, tm // bsz, D_MODEL)
        x_specs = [pl.BlockSpec(bt_block, lambda i: (0, 0, 0)), pl.BlockSpec(bt_block, lambda i: (0, nxt(i), 0))]
    else:
        x_specs = [pl.BlockSpec((tm, D_MODEL), lambda i: (i, 0)), pl.BlockSpec((tm, D_MODEL), lambda i: (nxt(i), 0))]
    in_specs = x_specs + [
        pl.BlockSpec((None, None, 3, bsz, D_MODEL), lambda i: (layer, sub, 0, 0, 0)),
        pl.BlockSpec((None, 1, D_MODEL), lambda i: (3 * layer + sub, 0, 0)),
        pl.BlockSpec((None, D_MODEL, D_FF), lambda i: (layer, 0, 0)),
        pl.BlockSpec((None, D_MODEL, D_FF), lambda i: (layer, 0, 0)),
        pl.BlockSpec((None, D_FF, D_MODEL), lambda i: (layer, 0, 0)),
    ]
    args = [x, x, mod, norm_g, wg, wu, wd]
    if final:
        in_specs.append(pl.BlockSpec((1, D_MODEL), lambda i: (0, 0)))
        args.append(final_g)
    return pl.pallas_call(
        functools.partial(_ffn_body, bsz, in_bt, final),
        grid=(steps,),
        in_specs=in_specs,
        out_specs=pl.BlockSpec((tm, D_MODEL), lambda i: (i, 0)),
        out_shape=jax.ShapeDtypeStruct((n, D_MODEL), F32),
        scratch_shapes=[pltpu.VMEM((2, tm, D_MODEL), BF16)] + ([pltpu.VMEM((2, tm, D_MODEL), F32)] if in_bt else []),
        compiler_params=_params(("arbitrary",)),
        name="ffn",
    )(*args)


def _pool_sums(e, bsz, tl):
    p2 = e[0:(tl + 15) * bsz] + e[bsz:(tl + 16) * bsz]
    p4 = p2[0:(tl + 13) * bsz] + p2[2 * bsz:(tl + 15) * bsz]
    p8 = p4[0:(tl + 9) * bsz] + p4[4 * bsz:(tl + 13) * bsz]
    p16 = p8[0:tl * bsz] + p8[8 * bsz:(tl + 8) * bsz]
    n = tl * bsz
    return {2: p2[7 * bsz:7 * bsz + n], 4: p4[6 * bsz:6 * bsz + n], 8: p8[4 * bsz:4 * bsz + n], 16: p16}


def _lru_gate_tile(xc, xcb, kb, wgt_ref, bgt_ref, lam_ref):
    sl = slice(kb * MXU_N, (kb + 1) * MXU_N)
    z = _dot(xcb[:, sl], wgt_ref[kb]) + bgt_ref[kb]
    tr = jnp.tanh(0.5 * z[:, :MXU_N])
    ti = jnp.tanh(0.5 * z[:, MXU_N:])
    half_rate = (0.5 * LRU_C) * jax.nn.log_sigmoid(lam_ref[:, sl])
    log_a = half_rate + half_rate * tr
    a = jnp.exp(log_a)
    y = jnp.tanh(log_a) * (-1.0 - a * a)
    b = jnp.exp2(0.5 * jnp.log2(y)) * ((0.5 + 0.5 * ti) * xc[:, sl])
    return a, b


def _lru_gates(xc, xcb, wgt_ref, bgt_ref, lam_ref):
    return [_lru_gate_tile(xc, xcb, kb, wgt_ref, bgt_ref, lam_ref) for kb in range(N_LRU_TILES)]


def _mixfwd_body(bsz, tl, pool_halo, x_ref, mod_ref, g_ref, win_ref, icnt_ref, cw_ref, cb_ref, wp_ref, ps_ref,
                 wgt_ref, bgt_ref, lam_ref, h0_ref, pm_ref, gg_ref, xc_ref, yf_ref, st_ref,
                 bufp_ref, bufr_ref, h_ref, hb_ref, ys_ref):
    i = pl.program_id(0)
    nt = pl.num_programs(0) - 1
    n = tl * bsz
    ph, cbk, cfw = POOL_HALO * bsz, CONV_BACK * bsz, CONV_FWD * bsz

    @pl.when(i == 0)
    def _():
        bufp_ref[...] = jnp.zeros(bufp_ref.shape, F32)
        bufr_ref[...] = jnp.zeros(bufr_ref.shape, F32)
        h_ref[...] = h0_ref[...]

    hb_ref[...] = _modulated(x_ref[...], g_ref[...], mod_ref[0], mod_ref[1], bsz)
    ur = _dot(hb_ref[...], win_ref[:, D_POOL:D_POOL + D_LRU])
    up = _dot(hb_ref[...], win_ref[:, :D_POOL])
    has_next = i < nt
    next_r = jnp.where(has_next, ur[0:cfw], 0.0)
    if pool_halo:
        next_p = jnp.where(has_next, up[0:ph], 0.0)
    else:
        next_p = jnp.zeros((ph, D_POOL), F32)

    pooled = []
    for g, w in enumerate(POOL_WINDOWS):
        sl = slice(g * POOL_GROUP_W, (g + 1) * POOL_GROUP_W)
        e = jnp.concatenate([bufp_ref[:, sl], next_p[:, sl]], axis=0)
        s = _pool_sums(e, bsz, tl)[w]
        pooled.append(s * icnt_ref[:, sl] - e[ph:ph + n])
    pooled = jnp.concatenate(pooled, axis=-1).astype(BF16)
    mixed = [_dot(pooled[:, k * MXU_N:(k + 1) * MXU_N], wp_ref[k]) for k in range(D_POOL // MXU_N)]
    pm_ref[...] = (jnp.concatenate(mixed, axis=-1) * ps_ref[...]).astype(BF16)

    ext = jnp.concatenate([bufr_ref[...], next_r], axis=0)
    xc = cb_ref[...] + ext[0:n] * cw_ref[0:1]
    for k in range(1, CONV_W):
        xc = xc + ext[k * bsz:k * bsz + n] * cw_ref[k:k + 1]
    xcb = xc.astype(BF16)
    xc_ref[...] = xcb

    ab = _lru_gates(xc, xcb, wgt_ref, bgt_ref, lam_ref)
    hs = [jnp.where(i <= 1, h0_ref[:, kb * MXU_N:(kb + 1) * MXU_N], h_ref[:, kb * MXU_N:(kb + 1) * MXU_N])
          for kb in range(N_LRU_TILES)]
    for t in range(tl):
        rows = slice(t * bsz, (t + 1) * bsz)
        for kb, (a, b) in enumerate(ab):
            hs[kb] = a[rows] * hs[kb] + b[rows]
            ys_ref[rows, kb * MXU_N:(kb + 1) * MXU_N] = hs[kb]
    yf_ref[...] = ys_ref[...].astype(BF16)
    hlast = jnp.concatenate(hs, axis=-1)
    h_ref[...] = hlast
    st_ref[...] = hlast

    ug = _dot(hb_ref[...], win_ref[:, D_POOL + D_LRU:])
    gg_ref[...] = jax.nn.gelu(ug, approximate=True).astype(BF16)

    if pool_halo:
        bufp_ref[0:ph] = bufp_ref[n:n + ph]
    bufp_ref[ph:ph + n] = up
    bufr_ref[0:cbk] = bufr_ref[n:n + cbk]
    bufr_ref[cbk:cbk + n] = ur


def _mixfwd_call(x, mod, norm_g, w_in, icnt, conv_w, conv_b, wp, ps, wgt, bgt, lam, h0, *, layer, bsz, tl, pool_halo):
    n_rows = x.shape[0]
    n = tl * bsz
    nt = n_rows // n
    cur = lambda i: (jnp.minimum(i, nt - 1), 0)
    prev = lambda i: (jnp.maximum(i - 1, 0), 0)
    return pl.pallas_call(
        functools.partial(_mixfwd_body, bsz, tl, pool_halo),
        grid=(nt + 1,),
        in_specs=[
            pl.BlockSpec((n, D_MODEL), cur),
            pl.BlockSpec((None, None, 3, bsz, D_MODEL), lambda i: (layer, 1, 0, 0, 0)),
            pl.BlockSpec((None, 1, D_MODEL), lambda i: (3 * layer + 1, 0, 0)),
            pl.BlockSpec((None, D_MODEL, D_IN), lambda i: (layer, 0, 0)),
            pl.BlockSpec((n, D_POOL), prev if pool_halo else (lambda i: (0, 0))),
            pl.BlockSpec((None, CONV_W, D_LRU), lambda i: (layer, 0, 0)),
            pl.BlockSpec((None, 1, D_LRU), lambda i: (layer, 0, 0)),
            pl.BlockSpec((None, D_POOL // MXU_N, MXU_N, MXU_N), lambda i: (layer, 0, 0, 0)),
            pl.BlockSpec((None, 1, D_POOL), lambda i: (layer, 0, 0)),
            pl.BlockSpec((None, None, N_LRU_TILES, MXU_N, 2 * MXU_N), lambda i: (layer, 0, 0, 0, 0)),
            pl.BlockSpec((None, None, N_LRU_TILES, 1, 2 * MXU_N), lambda i: (layer, 0, 0, 0, 0)),
            pl.BlockSpec((None, None, 1, D_LRU), lambda i: (layer, 0, 0, 0)),
            pl.BlockSpec((None, bsz, D_LRU), lambda i: (0, 0, 0)),
        ],
        out_specs=[
            pl.BlockSpec((n, D_POOL), prev),
            pl.BlockSpec((n, D_LRU), cur),
            pl.BlockSpec((n, D_LRU), prev),
            pl.BlockSpec((n, D_LRU), prev),
            pl.BlockSpec((bsz, D_LRU), lambda i: (0, 0)),
        ],
        out_shape=[
            jax.ShapeDtypeStruct((n_rows, D_POOL), BF16),
            jax.ShapeDtypeStruct((n_rows, D_LRU), BF16),
            jax.ShapeDtypeStruct((n_rows, D_LRU), BF16),
            jax.ShapeDtypeStruct((n_rows, D_LRU), BF16),
            jax.ShapeDtypeStruct((bsz, D_LRU), F32),
        ],
        scratch_shapes=[
            pltpu.VMEM(((POOL_HALO + tl) * bsz, D_POOL), F32),
            pltpu.VMEM(((CONV_BACK + tl) * bsz, D_LRU), F32),
            pltpu.VMEM((bsz, D_LRU), F32),
            pltpu.VMEM((n, D_MODEL), BF16),
            pltpu.VMEM((n, D_LRU), F32),
        ],
        compiler_params=_params(("arbitrary",)),
        name="mixfwd",
    )(x, mod, norm_g, w_in, icnt, conv_w, conv_b, wp, ps, wgt, bgt, lam, h0)


def _mixbwd_body(bsz, tl, x_ref, mod_ref, pm_ref, gg_ref, xc_ref, yf_ref, wgt_ref, bgt_ref, lam_ref, h0_ref, wout_ref,
                 o_ref, st_ref, rec_ref, lhs_ref, yb_ref, h_ref):
    i = pl.program_id(0)
    nt = pl.num_programs(0) - 1
    gate = mod_ref[0]

    @pl.when(i == 0)
    def _():
        rec_ref[...] = jnp.zeros(rec_ref.shape, BF16)
        h_ref[...] = h0_ref[...]

    lhs_ref[:, :D_POOL] = pm_ref[...]
    lhs_ref[:, D_POOL:] = rec_ref[...]

    xcb = xc_ref[...]
    ab = _lru_gates(xcb.astype(F32), xcb, wgt_ref, bgt_ref, lam_ref)
    hs = [h_ref[:, kb * MXU_N:(kb + 1) * MXU_N] for kb in range(N_LRU_TILES)]
    for t in reversed(range(tl)):
        rows = slice(t * bsz, (t + 1) * bsz)
        for kb, (a, b) in enumerate(ab):
            sl = slice(kb * MXU_N, (kb + 1) * MXU_N)
            hs[kb] = a[rows] * hs[kb] + b[rows]
            yb_ref[rows, sl] = hs[kb]
    rec_ref[...] = ((yf_ref[...].astype(F32) + yb_ref[...]) * gg_ref[...].astype(F32)).astype(BF16)
    hlast = jnp.concatenate(hs, axis=-1)
    h_ref[...] = hlast

    y = _dot(lhs_ref[...], wout_ref[...])
    o_ref[...] = x_ref[...] + _per_batch(y, bsz, lambda v: v * gate[None])

    @pl.when(i == nt - 1)
    def _():
        st_ref[...] = hlast


def _mixbwd_call(x, mod, pm, gg, xc, yf, wgt, bgt, lam, h0, w_out, *, layer, bsz, tl):
    n_rows = x.shape[0]
    n = tl * bsz
    nt = n_rows // n
    cur = lambda i: (jnp.maximum(nt - 1 - i, 0), 0)
    prev = lambda i: (jnp.minimum(nt - i, nt - 1), 0)
    return pl.pallas_call(
        functools.partial(_mixbwd_body, bsz, tl),
        grid=(nt + 1,),
        in_specs=[
            pl.BlockSpec((n, D_MODEL), prev),
            pl.BlockSpec((None, None, 1, bsz, D_MODEL), lambda i: (layer, 1, 2, 0, 0)),
            pl.BlockSpec((n, D_POOL), prev),
            pl.BlockSpec((n, D_LRU), cur),
            pl.BlockSpec((n, D_LRU), cur),
            pl.BlockSpec((n, D_LRU), cur),
            pl.BlockSpec((None, None, N_LRU_TILES, MXU_N, 2 * MXU_N), lambda i: (layer, 1, 0, 0, 0)),
            pl.BlockSpec((None, None, N_LRU_TILES, 1, 2 * MXU_N), lambda i: (layer, 1, 0, 0, 0)),
            pl.BlockSpec((None, None, 1, D_LRU), lambda i: (layer, 1, 0, 0)),
            pl.BlockSpec((None, bsz, D_LRU), lambda i: (1, 0, 0)),
            pl.BlockSpec((None, D_MIX, D_MODEL), lambda i: (layer, 0, 0)),
        ],
        out_specs=[
            pl.BlockSpec((n, D_MODEL), prev),
            pl.BlockSpec((bsz, D_LRU), lambda i: (0, 0)),
        ],
        out_shape=[
            jax.ShapeDtypeStruct((n_rows, D_MODEL), F32),
            jax.ShapeDtypeStruct((bsz, D_LRU), F32),
        ],
        scratch_shapes=[
            pltpu.VMEM((n, D_LRU), BF16),
            pltpu.VMEM((n, D_MIX), BF16),
            pltpu.VMEM((n, D_LRU), F32),
            pltpu.VMEM((bsz, D_LRU), F32),
        ],
        compiler_params=_params(("arbitrary",)),
        name="mixbwd",
    )(x, mod, pm, gg, xc, yf, wgt, bgt, lam, h0, w_out)


def _inv_counts(n_pos, bsz):
    t = np.arange(n_pos)
    cols = []
    for w in POOL_WINDOWS:
        cnt = np.clip(t + w // 2, 0, n_pos) - np.clip(t - w // 2, 0, n_pos)
        cols.append(np.repeat((1.0 / cnt)[:, None], POOL_GROUP_W, axis=1))
    tab = np.concatenate(cols, axis=1).astype(np.float32)
    return jnp.asarray(np.repeat(tab, bsz, axis=0))


def _block_diag(w):
    heads, hw = w.shape[-3], w.shape[-1]
    eye = jnp.eye(heads, dtype=w.dtype)
    full = jnp.einsum('...hij,hg->...higj', w, eye)
    return full.reshape(w.shape[:-3] + (heads * hw, heads * hw))


def kernel(x_prompt, x_sample, state_lru, c, c_ctx, norm_g, w_ada, b_ada, ffn1_gate, ffn1_up, ffn1_down,
           w_in, conv_w, conv_b, w_pool, pool_scale, lru_w_r, lru_b_r, lru_w_i, lru_b_i, lru_lambda,
           w_out, ffn2_gate, ffn2_up, ffn2_down, final_g):
    b_ctx, l_ctx, _ = x_prompt.shape
    b_lat, l_lat, _ = x_sample.shape

    cmat = jnp.zeros((MOD_ROWS, D_MODEL), F32).at[:b_lat].set(c).at[b_lat].set(c_ctx)
    mod = _ada_call(cmat, w_ada, b_ada).reshape(DEPTH, MOD_ROWS, N_MOD, D_MODEL)
    mod_lat = jnp.transpose(mod[:, :b_lat], (0, 2, 1, 3)).reshape(DEPTH, 3, 3, b_lat, D_MODEL)
    mod_ctx = jnp.broadcast_to(mod[:, b_lat][:, :, None, :], (DEPTH, N_MOD, b_ctx, D_MODEL))
    mod_ctx = mod_ctx.reshape(DEPTH, 3, 3, b_ctx, D_MODEL)

    bf = lambda w: w.astype(BF16)
    f1g, f1u, f1d, f2g, f2u, f2d = (bf(w) for w in (ffn1_gate, ffn1_up, ffn1_down, ffn2_gate, ffn2_up, ffn2_down))
    w_in_b, w_out_b = bf(w_in), bf(w_out)
    wp = bf(_block_diag(w_pool.reshape(DEPTH, D_POOL // MXU_N, MXU_N // POOL_GROUP_W, POOL_GROUP_W, POOL_GROUP_W)))
    hpt = MXU_N // LRU_HEAD_W
    wr = _block_diag(lru_w_r.reshape(DEPTH, 2, N_LRU_TILES, hpt, LRU_HEAD_W, LRU_HEAD_W))
    wi = _block_diag(lru_w_i.reshape(DEPTH, 2, N_LRU_TILES, hpt, LRU_HEAD_W, LRU_HEAD_W))
    wgt = bf(jnp.concatenate([wr, wi], axis=-1))
    bgt = jnp.concatenate([lru_b_r.reshape(DEPTH, 2, N_LRU_TILES, 1, MXU_N),
                           lru_b_i.reshape(DEPTH, 2, N_LRU_TILES, 1, MXU_N)], axis=-1)
    lam = lru_lambda.reshape(DEPTH, 2, 1, D_LRU)
    norm_g3 = norm_g.reshape(DEPTH * 3, 1, D_MODEL)
    conv_b3 = conv_b.reshape(DEPTH, 1, D_LRU)
    ps3 = pool_scale.reshape(DEPTH, 1, D_POOL)
    fg = final_g.reshape(1, D_MODEL)

    streams = {
        'ctx': dict(x=x_prompt, mod=mod_ctx, bsz=b_ctx,
                    tl=TOKEN_TILE // b_ctx, pool_halo=True, icnt=_inv_counts(l_ctx, b_ctx)),
        'lat': dict(x=x_sample, mod=mod_lat, bsz=b_lat,
                    tl=GRID_W, pool_halo=False, icnt=_inv_counts(GRID_W, b_lat)),
    }
    new_states = []
    for l in range(DEPTH):
        for name, s in streams.items():
            bsz, tl = s['bsz'], s['tl']
            h0 = (jnp.zeros((2, bsz, D_LRU), F32) if name == 'ctx'
                  else jnp.transpose(state_lru[:, l], (1, 0, 2)))
            x = _ffn_call(s['x'], s['mod'], norm_g3, f1g, f1u, f1d, None, layer=l, sub=0, bsz=bsz)
            pm, gg, xc, yf, st_f = _mixfwd_call(x, s['mod'], norm_g3, w_in_b, s['icnt'], conv_w, conv_b3, wp, ps3,
                                                wgt, bgt, lam, h0, layer=l, bsz=bsz, tl=tl, pool_halo=s['pool_halo'])
            x, st_b = _mixbwd_call(x, s['mod'], pm, gg, xc, yf, wgt, bgt, lam, h0, w_out_b, layer=l, bsz=bsz, tl=tl)
            s['x'] = _ffn_call(x, s['mod'], norm_g3, f2g, f2u, f2d, fg if l == DEPTH - 1 else None,
                               layer=l, sub=2, bsz=bsz)
            if name == 'ctx':
                new_states.append(jnp.stack([st_f, st_b], axis=0))
    y_prompt = jnp.transpose(streams['ctx']['x'].reshape(l_ctx, b_ctx, D_MODEL), (1, 0, 2))
    y_sample = jnp.transpose(streams['lat']['x'].reshape(l_lat, b_lat, D_MODEL), (1, 0, 2))
    new_state_lru = jnp.transpose(jnp.stack(new_states, axis=0), (2, 0, 1, 3))
    return (y_prompt, y_sample, new_state_lru)
```

```python
import functools

import numpy as np
import jax
import jax.numpy as jnp
from jax import lax
from jax.experimental import pallas as pl
from jax.experimental.pallas import tpu as pltpu

D_MODEL = 1024
DEPTH = 4
GRID_W = 64
D_POOL = 512
N_POOL_GROUPS = 4
POOL_GROUP_W = D_POOL // N_POOL_GROUPS
POOL_WINDOWS = (2, 4, 8, 16)
POOL_HALO = max(POOL_WINDOWS) // 2
D_LRU = 512
N_LRU_HEADS = 8
LRU_HEAD_W = D_LRU // N_LRU_HEADS
CONV_W = 4
CONV_BACK = CONV_W // 2
CONV_FWD = CONV_W - 1 - CONV_BACK
LRU_C = 8.0
D_MIX = D_POOL + D_LRU
D_IN = D_POOL + 2 * D_LRU
D_FF = 2816
N_MOD = 9
EPS = 1e-6

MXU_N = 256
FF_CHUNK = MXU_N
N_FF_CHUNKS = D_FF // FF_CHUNK
N_LRU_TILES = D_LRU // MXU_N
MOD_ROWS = 16
ADA_NB = 1152
TOKEN_TILE = 512
FFN_TILE = 1024
FFN_PIECE = 128
VMEM_LIMIT = 56 * 1024 * 1024

BF16 = jnp.bfloat16
F32 = jnp.float32


def _dot(a, b):
    return jnp.dot(a, b, preferred_element_type=F32)


def _rms_norm(x, g):
    ms = jnp.mean(x * x, axis=-1, keepdims=True)
    return x * lax.rsqrt(ms + EPS) * g


def _per_batch(x, bsz, fn):
    n, d = x.shape
    return fn(x.reshape(n // bsz, bsz, d)).reshape(n, d)


def _modulated(x, g, shift, scale, bsz):
    return _per_batch(_rms_norm(x, g), bsz, lambda v: v * (1.0 + scale)[None] + shift[None]).astype(BF16)


def _params(sem):
    return pltpu.CompilerParams(dimension_semantics=sem, vmem_limit_bytes=VMEM_LIMIT)


def _ada_body(c_ref, w_ref, b_ref, o_ref):
    c = c_ref[...]
    s = (c * jax.nn.sigmoid(c)).astype(BF16)
    o_ref[...] = _dot(s, w_ref[...].astype(BF16)) + b_ref[...]


def _ada_call(cmat, w_ada, b_ada):
    nb = (N_MOD * D_MODEL) // ADA_NB
    return pl.pallas_call(
        _ada_body,
        grid=(DEPTH, nb),
        in_specs=[
            pl.BlockSpec((MOD_ROWS, D_MODEL), lambda l, j: (0, 0)),
            pl.BlockSpec((None, D_MODEL, ADA_NB), lambda l, j: (l, 0, j)),
            pl.BlockSpec((None, 1, ADA_NB), lambda l, j: (l, 0, j)),
        ],
        out_specs=pl.BlockSpec((None, MOD_ROWS, ADA_NB), lambda l, j: (l, 0, j)),
        out_shape=jax.ShapeDtypeStruct((DEPTH, MOD_ROWS, N_MOD * D_MODEL), F32),
        compiler_params=_params(("arbitrary", "arbitrary")),
        name="adaln",
    )(cmat, w_ada, b_ada.reshape(DEPTH, 1, N_MOD * D_MODEL))


def _ffn_body(bsz, in_bt, final, x_ref, xnext_ref, mod_ref, g_ref, wg_ref, wu_ref, wd_ref, *rest):
    rest = list(rest)
    xt_ref = rest.pop() if in_bt else None
    if final:
        fg_ref, o_ref, hb_ref = rest
    else:
        o_ref, hb_ref = rest
    i = pl.program_id(0)
    slot = i % 2
    shift, scale, gate = mod_ref[0], mod_ref[1], mod_ref[2]
    n_pieces = FFN_TILE // FFN_PIECE
    pt = FFN_PIECE // bsz

    def prep(src_ref, dst_slot, p):
        rows = slice(p * FFN_PIECE, (p + 1) * FFN_PIECE)
        if in_bt:
            x = jnp.swapaxes(src_ref[:, p * pt:(p + 1) * pt, :], 0, 1).reshape(FFN_PIECE, D_MODEL)
            xt_ref[dst_slot, rows] = x
        else:
            x = src_ref[rows]
        hb_ref[dst_slot, rows] = _modulated(x, g_ref[...], shift, scale, bsz)

    def gated(v):
        return _per_batch(v, bsz, lambda t: t * (0.5 * gate)[None])

    @pl.when(i == 0)
    def _():
        for p in range(n_pieces):
            prep(x_ref, 0, p)

    def gate_up(c):
        sl = slice(c * FF_CHUNK, (c + 1) * FF_CHUNK)
        return _dot(hb_ref[slot], wg_ref[:, sl]), _dot(hb_ref[slot], wu_ref[:, sl])

    g, u = gate_up(0)
    for c in range(N_FF_CHUNKS):
        sl = slice(c * FF_CHUNK, (c + 1) * FF_CHUNK)
        a = ((g * jax.nn.sigmoid(g)) * u).astype(BF16)
        if c + 1 < N_FF_CHUNKS:
            g, u = gate_up(c + 1)
        part = gated(_dot(a, wd_ref[sl, :]))
        if c == 0:
            o_ref[...] = (xt_ref[slot] if in_bt else x_ref[...]) + part
        else:
            o_ref[...] += part
        if c < n_pieces:
            prep(xnext_ref, 1 - slot, c)
    if final:
        o_ref[...] = _rms_norm(o_ref[...], fg_ref[...])


def _ffn_call(x, mod, norm_g, wg, wu, wd, final_g, *, layer, sub, bsz):
    tm = FFN_TILE
    in_bt = x.ndim == 3
    n = x.shape[0] * x.shape[1] if in_bt else x.shape[0]
    steps = n // tm
    final = final_g is not None
    nxt = lambda i: jnp.minimum(i + 1, steps - 1)
    if in_bt:
        bt_block = (bsz, tm // bsz, D_MODEL)
        x_specs = [pl.BlockSpec(bt_block, lambda i: (0, 0, 0)), pl.BlockSpec(bt_block, lambda i: (0, nxt(i), 0))]
    else:
        x_specs = [pl.BlockSpec((tm, D_MODEL), lambda i: (i, 0)), pl.BlockSpec((tm, D_MODEL), lambda i: (nxt(i), 0))]
    in_specs = x_specs + [
        pl.BlockSpec((None, None, 3, bsz, D_MODEL), lambda i: (layer, sub, 0, 0, 0)),
        pl.BlockSpec((None, 1, D_MODEL), lambda i: (3 * layer + sub, 0, 0)),
        pl.BlockSpec((None, D_MODEL, D_FF), lambda i: (layer, 0, 0)),
        pl.BlockSpec((None, D_MODEL, D_FF), lambda i: (layer, 0, 0)),
        pl.BlockSpec((None, D_FF, D_MODEL), lambda i: (layer, 0, 0)),
    ]
    args = [x, x, mod, norm_g, wg, wu, wd]
    if final:
        in_specs.append(pl.BlockSpec((1, D_MODEL), lambda i: (0, 0)))
        args.append(final_g)
    return pl.pallas_call(
        functools.partial(_ffn_body, bsz, in_bt, final),
        grid=(steps,),
        in_specs=in_specs,
        out_specs=pl.BlockSpec((tm, D_MODEL), lambda i: (i, 0)),
        out_shape=jax.ShapeDtypeStruct((n, D_MODEL), F32),
        scratch_shapes=[pltpu.VMEM((2, tm, D_MODEL), BF16)] + ([pltpu.VMEM((2, tm, D_MODEL), F32)] if in_bt else []),
        compiler_params=_params(("arbitrary",)),
        name="ffn",
    )(*args)


def _pool_sums(e, bsz, tl):
    p2 = e[0:(tl + 15) * bsz] + e[bsz:(tl + 16) * bsz]
    p4 = p2[0:(tl + 13) * bsz] + p2[2 * bsz:(tl + 15) * bsz]
    p8 = p4[0:(tl + 9) * bsz] + p4[4 * bsz:(tl + 13) * bsz]
    p16 = p8[0:tl * bsz] + p8[8 * bsz:(tl + 8) * bsz]
    n = tl * bsz
    return {2: p2[7 * bsz:7 * bsz + n], 4: p4[6 * bsz:6 * bsz + n], 8: p8[4 * bsz:4 * bsz + n], 16: p16}


def _lru_gate_tile(xc, xcb, kb, wgt_ref, bgt_ref, lam_ref):
    sl = slice(kb * MXU_N, (kb + 1) * MXU_N)
    z = _dot(xcb[:, sl], wgt_ref[kb]) + bgt_ref[kb]
    tr = jnp.tanh(0.5 * z[:, :MXU_N])
    ti = jnp.tanh(0.5 * z[:, MXU_N:])
    half_rate = (0.5 * LRU_C) * jax.nn.log_sigmoid(lam_ref[:, sl])
    log_a = half_rate + half_rate * tr
    a = jnp.exp(log_a)
    y = jnp.tanh(log_a) * (-1.0 - a * a)
    b = jnp.exp2(0.5 * jnp.log2(y)) * ((0.5 + 0.5 * ti) * xc[:, sl])
    return a, b


def _lru_gates(xc, xcb, wgt_ref, bgt_ref, lam_ref):
    return [_lru_gate_tile(xc, xcb, kb, wgt_ref, bgt_ref, lam_ref) for kb in range(N_LRU_TILES)]


def _mixfwd_body(bsz, tl, pool_halo, x_ref, mod_ref, g_ref, win_ref, icnt_ref, cw_ref, cb_ref, wp_ref, ps_ref,
                 wgt_ref, bgt_ref, lam_ref, h0_ref, pm_ref, gg_ref, xc_ref, yf_ref, st_ref,
                 bufp_ref, bufr_ref, h_ref, hb_ref, ys_ref):
    i = pl.program_id(0)
    nt = pl.num_programs(0) - 1
    n = tl * bsz
    ph, cbk, cfw = POOL_HALO * bsz, CONV_BACK * bsz, CONV_FWD * bsz

    @pl.when(i == 0)
    def _():
        bufp_ref[...] = jnp.zeros(bufp_ref.shape, F32)
        bufr_ref[...] = jnp.zeros(bufr_ref.shape, F32)
        h_ref[...] = h0_ref[...]

    hb_ref[...] = _modulated(x_ref[...], g_ref[...], mod_ref[0], mod_ref[1], bsz)
    ur = _dot(hb_ref[...], win_ref[:, D_POOL:D_POOL + D_LRU])
    up = _dot(hb_ref[...], win_ref[:, :D_POOL])
    has_next = i < nt
    next_r = jnp.where(has_next, ur[0:cfw], 0.0)
    if pool_halo:
        next_p = jnp.where(has_next, up[0:ph], 0.0)
    else:
        next_p = jnp.zeros((ph, D_POOL), F32)

    pooled = []
    for g, w in enumerate(POOL_WINDOWS):
        sl = slice(g * POOL_GROUP_W, (g + 1) * POOL_GROUP_W)
        e = jnp.concatenate([bufp_ref[:, sl], next_p[:, sl]], axis=0)
        s = _pool_sums(e, bsz, tl)[w]
        pooled.append(s * icnt_ref[:, sl] - e[ph:ph + n])
    pooled = jnp.concatenate(pooled, axis=-1).astype(BF16)
    mixed = [_dot(pooled[:, k * MXU_N:(k + 1) * MXU_N], wp_ref[k]) for k in range(D_POOL // MXU_N)]
    pm_ref[...] = (jnp.concatenate(mixed, axis=-1) * ps_ref[...]).astype(BF16)

    ext = jnp.concatenate([bufr_ref[...], next_r], axis=0)
    xc = cb_ref[...] + ext[0:n] * cw_ref[0:1]
    for k in range(1, CONV_W):
        xc = xc + ext[k * bsz:k * bsz + n] * cw_ref[k:k + 1]
    xcb = xc.astype(BF16)
    xc_ref[...] = xcb

    ab = _lru_gates(xc, xcb, wgt_ref, bgt_ref, lam_ref)
    hs = [jnp.where(i <= 1, h0_ref[:, kb * MXU_N:(kb + 1) * MXU_N], h_ref[:, kb * MXU_N:(kb + 1) * MXU_N])
          for kb in range(N_LRU_TILES)]
    for t in range(tl):
        rows = slice(t * bsz, (t + 1) * bsz)
        for kb, (a, b) in enumerate(ab):
            hs[kb] = a[rows] * hs[kb] + b[rows]
            ys_ref[rows, kb * MXU_N:(kb + 1) * MXU_N] = hs[kb]
    yf_ref[...] = ys_ref[...].astype(BF16)
    hlast = jnp.concatenate(hs, axis=-1)
    h_ref[...] = hlast
    st_ref[...] = hlast

    gg_ref[...] = _dot(hb_ref[...], win_ref[:, D_POOL + D_LRU:]).astype(BF16)

    if pool_halo:
        bufp_ref[0:ph] = bufp_ref[n:n + ph]
    bufp_ref[ph:ph + n] = up
    bufr_ref[0:cbk] = bufr_ref[n:n + cbk]
    bufr_ref[cbk:cbk + n] = ur


def _mixfwd_call(x, mod, norm_g, w_in, icnt, conv_w, conv_b, wp, ps, wgt, bgt, lam, h0, *, layer, bsz, tl, pool_halo):
    n_rows = x.shape[0]
    n = tl * bsz
    nt = n_rows // n
    cur = lambda i: (jnp.minimum(i, nt - 1), 0)
    prev = lambda i: (jnp.maximum(i - 1, 0), 0)
    return pl.pallas_call(
        functools.partial(_mixfwd_body, bsz, tl, pool_halo),
        grid=(nt + 1,),
        in_specs=[
            pl.BlockSpec((n, D_MODEL), cur),
            pl.BlockSpec((None, None, 3, bsz, D_MODEL), lambda i: (layer, 1, 0, 0, 0)),
            pl.BlockSpec((None, 1, D_MODEL), lambda i: (3 * layer + 1, 0, 0)),
            pl.BlockSpec((None, D_MODEL, D_IN), lambda i: (layer, 0, 0)),
            pl.BlockSpec((n, D_POOL), prev if pool_halo else (lambda i: (0, 0))),
            pl.BlockSpec((None, CONV_W, D_LRU), lambda i: (layer, 0, 0)),
            pl.BlockSpec((None, 1, D_LRU), lambda i: (layer, 0, 0)),
            pl.BlockSpec((None, D_POOL // MXU_N, MXU_N, MXU_N), lambda i: (layer, 0, 0, 0)),
            pl.BlockSpec((None, 1, D_POOL), lambda i: (layer, 0, 0)),
            pl.BlockSpec((None, None, N_LRU_TILES, MXU_N, 2 * MXU_N), lambda i: (layer, 0, 0, 0, 0)),
            pl.BlockSpec((None, None, N_LRU_TILES, 1, 2 * MXU_N), lambda i: (layer, 0, 0, 0, 0)),
            pl.BlockSpec((None, None, 1, D_LRU), lambda i: (layer, 0, 0, 0)),
            pl.BlockSpec((None, bsz, D_LRU), lambda i: (0, 0, 0)),
        ],
        out_specs=[
            pl.BlockSpec((n, D_POOL), prev),
            pl.BlockSpec((n, D_LRU), cur),
            pl.BlockSpec((n, D_LRU), prev),
            pl.BlockSpec((n, D_LRU), prev),
            pl.BlockSpec((bsz, D_LRU), lambda i: (0, 0)),
        ],
        out_shape=[
            jax.ShapeDtypeStruct((n_rows, D_POOL), BF16),
            jax.ShapeDtypeStruct((n_rows, D_LRU), BF16),
            jax.ShapeDtypeStruct((n_rows, D_LRU), BF16),
            jax.ShapeDtypeStruct((n_rows, D_LRU), BF16),
            jax.ShapeDtypeStruct((bsz, D_LRU), F32),
        ],
        scratch_shapes=[
            pltpu.VMEM(((POOL_HALO + tl) * bsz, D_POOL), F32),
            pltpu.VMEM(((CONV_BACK + tl) * bsz, D_LRU), F32),
            pltpu.VMEM((bsz, D_LRU), F32),
            pltpu.VMEM((n, D_MODEL), BF16),
            pltpu.VMEM((n, D_LRU), F32),
        ],
        compiler_params=_params(("arbitrary",)),
        name="mixfwd",
    )(x, mod, norm_g, w_in, icnt, conv_w, conv_b, wp, ps, wgt, bgt, lam, h0)


def _mixbwd_body(bsz, tl, x_ref, mod_ref, pm_ref, gg_ref, xc_ref, yf_ref, wgt_ref, bgt_ref, lam_ref, h0_ref, wout_ref,
                 o_ref, st_ref, rec_ref, lhs_ref, yb_ref, h_ref):
    i = pl.program_id(0)
    nt = pl.num_programs(0) - 1
    gate = mod_ref[0]

    @pl.when(i == 0)
    def _():
        rec_ref[...] = jnp.zeros(rec_ref.shape, BF16)
        h_ref[...] = h0_ref[...]

    lhs_ref[:, :D_POOL] = pm_ref[...]
    lhs_ref[:, D_POOL:] = rec_ref[...]

    xcb = xc_ref[...]
    ab = _lru_gates(xcb.astype(F32), xcb, wgt_ref, bgt_ref, lam_ref)
    hs = [h_ref[:, kb * MXU_N:(kb + 1) * MXU_N] for kb in range(N_LRU_TILES)]
    for t in reversed(range(tl)):
        rows = slice(t * bsz, (t + 1) * bsz)
        for kb, (a, b) in enumerate(ab):
            sl = slice(kb * MXU_N, (kb + 1) * MXU_N)
            hs[kb] = a[rows] * hs[kb] + b[rows]
            yb_ref[rows, sl] = hs[kb]
    gated = jax.nn.gelu(gg_ref[...].astype(F32), approximate=True)
    rec_ref[...] = ((yf_ref[...].astype(F32) + yb_ref[...]) * gated).astype(BF16)
    hlast = jnp.concatenate(hs, axis=-1)
    h_ref[...] = hlast

    y = _dot(lhs_ref[...], wout_ref[...])
    o_ref[...] = x_ref[...] + _per_batch(y, bsz, lambda v: v * gate[None])

    @pl.when(i == nt - 1)
    def _():
        st_ref[...] = hlast


def _mixbwd_call(x, mod, pm, gg, xc, yf, wgt, bgt, lam, h0, w_out, *, layer, bsz, tl):
    n_rows = x.shape[0]
    n = tl * bsz
    nt = n_rows // n
    cur = lambda i: (jnp.maximum(nt - 1 - i, 0), 0)
    prev = lambda i: (jnp.minimum(nt - i, nt - 1), 0)
    return pl.pallas_call(
        functools.partial(_mixbwd_body, bsz, tl),
        grid=(nt + 1,),
        in_specs=[
            pl.BlockSpec((n, D_MODEL), prev),
            pl.BlockSpec((None, None, 1, bsz, D_MODEL), lambda i: (layer, 1, 2, 0, 0)),
            pl.BlockSpec((n, D_POOL), prev),
            pl.BlockSpec((n, D_LRU), cur),
            pl.BlockSpec((n, D_LRU), cur),
            pl.BlockSpec((n, D_LRU), cur),
            pl.BlockSpec((None, None, N_LRU_TILES, MXU_N, 2 * MXU_N), lambda i: (layer, 1, 0, 0, 0)),
            pl.BlockSpec((None, None, N_LRU_TILES, 1, 2 * MXU_N), lambda i: (layer, 1, 0, 0, 0)),
            pl.BlockSpec((None, None, 1, D_LRU), lambda i: (layer, 1, 0, 0)),
            pl.BlockSpec((None, bsz, D_LRU), lambda i: (1, 0, 0)),
            pl.BlockSpec((None, D_MIX, D_MODEL), lambda i: (layer, 0, 0)),
        ],
        out_specs=[
            pl.BlockSpec((n, D_MODEL), prev),
            pl.BlockSpec((bsz, D_LRU), lambda i: (0, 0)),
        ],
        out_shape=[
            jax.ShapeDtypeStruct((n_rows, D_MODEL), F32),
            jax.ShapeDtypeStruct((bsz, D_LRU), F32),
        ],
        scratch_shapes=[
            pltpu.VMEM((n, D_LRU), BF16),
            pltpu.VMEM((n, D_MIX), BF16),
            pltpu.VMEM((n, D_LRU), F32),
            pltpu.VMEM((bsz, D_LRU), F32),
        ],
        compiler_params=_params(("arbitrary",)),
        name="mixbwd",
    )(x, mod, pm, gg, xc, yf, wgt, bgt, lam, h0, w_out)


def _inv_counts(n_pos, bsz):
    t = np.arange(n_pos)
    cols = []
    for w in POOL_WINDOWS:
        cnt = np.clip(t + w // 2, 0, n_pos) - np.clip(t - w // 2, 0, n_pos)
        cols.append(np.repeat((1.0 / cnt)[:, None], POOL_GROUP_W, axis=1))
    tab = np.concatenate(cols, axis=1).astype(np.float32)
    return jnp.asarray(np.repeat(tab, bsz, axis=0))


def _block_diag(w):
    heads, hw = w.shape[-3], w.shape[-1]
    eye = jnp.eye(heads, dtype=w.dtype)
    full = jnp.einsum('...hij,hg->...higj', w, eye)
    return full.reshape(w.shape[:-3] + (heads * hw, heads * hw))


def kernel(x_prompt, x_sample, state_lru, c, c_ctx, norm_g, w_ada, b_ada, ffn1_gate, ffn1_up, ffn1_down,
           w_in, conv_w, conv_b, w_pool, pool_scale, lru_w_r, lru_b_r, lru_w_i, lru_b_i, lru_lambda,
           w_out, ffn2_gate, ffn2_up, ffn2_down, final_g):
    b_ctx, l_ctx, _ = x_prompt.shape
    b_lat, l_lat, _ = x_sample.shape

    cmat = jnp.zeros((MOD_ROWS, D_MODEL), F32).at[:b_lat].set(c).at[b_lat].set(c_ctx)
    mod = _ada_call(cmat, w_ada, b_ada).reshape(DEPTH, MOD_ROWS, N_MOD, D_MODEL)
    mod_lat = jnp.transpose(mod[:, :b_lat], (0, 2, 1, 3)).reshape(DEPTH, 3, 3, b_lat, D_MODEL)
    mod_ctx = jnp.broadcast_to(mod[:, b_lat][:, :, None, :], (DEPTH, N_MOD, b_ctx, D_MODEL))
    mod_ctx = mod_ctx.reshape(DEPTH, 3, 3, b_ctx, D_MODEL)

    bf = lambda w: w.astype(BF16)
    f1g, f1u, f1d, f2g, f2u, f2d = (bf(w) for w in (ffn1_gate, ffn1_up, ffn1_down, ffn2_gate, ffn2_up, ffn2_down))
    w_in_b, w_out_b = bf(w_in), bf(w_out)
    wp = bf(_block_diag(w_pool.reshape(DEPTH, D_POOL // MXU_N, MXU_N // POOL_GROUP_W, POOL_GROUP_W, POOL_GROUP_W)))
    hpt = MXU_N // LRU_HEAD_W
    wr = _block_diag(lru_w_r.reshape(DEPTH, 2, N_LRU_TILES, hpt, LRU_HEAD_W, LRU_HEAD_W))
    wi = _block_diag(lru_w_i.reshape(DEPTH, 2, N_LRU_TILES, hpt, LRU_HEAD_W, LRU_HEAD_W))
    wgt = bf(jnp.concatenate([wr, wi], axis=-1))
    bgt = jnp.concatenate([lru_b_r.reshape(DEPTH, 2, N_LRU_TILES, 1, MXU_N),
                           lru_b_i.reshape(DEPTH, 2, N_LRU_TILES, 1, MXU_N)], axis=-1)
    lam = lru_lambda.reshape(DEPTH, 2, 1, D_LRU)
    norm_g3 = norm_g.reshape(DEPTH * 3, 1, D_MODEL)
    conv_b3 = conv_b.reshape(DEPTH, 1, D_LRU)
    ps3 = pool_scale.reshape(DEPTH, 1, D_POOL)
    fg = final_g.reshape(1, D_MODEL)

    streams = {
        'ctx': dict(x=x_prompt, mod=mod_ctx, bsz=b_ctx,
                    tl=TOKEN_TILE // b_ctx, pool_halo=True, icnt=_inv_counts(l_ctx, b_ctx)),
        'lat': dict(x=x_sample, mod=mod_lat, bsz=b_lat,
                    tl=GRID_W, pool_halo=False, icnt=_inv_counts(GRID_W, b_lat)),
    }
    new_states = []
    for l in range(DEPTH):
        for name, s in streams.items():
            bsz, tl = s['bsz'], s['tl']
            h0 = (jnp.zeros((2, bsz, D_LRU), F32) if name == 'ctx'
                  else jnp.transpose(state_lru[:, l], (1, 0, 2)))
            x = _ffn_call(s['x'], s['mod'], norm_g3, f1g, f1u, f1d, None, layer=l, sub=0, bsz=bsz)
            pm, gg, xc, yf, st_f = _mixfwd_call(x, s['mod'], norm_g3, w_in_b, s['icnt'], conv_w, conv_b3, wp, ps3,
                                                wgt, bgt, lam, h0, layer=l, bsz=bsz, tl=tl, pool_halo=s['pool_halo'])
            x, st_b = _mixbwd_call(x, s['mod'], pm, gg, xc, yf, wgt, bgt, lam, h0, w_out_b, layer=l, bsz=bsz, tl=tl)
            s['x'] = _ffn_call(x, s['mod'], norm_g3, f2g, f2u, f2d, fg if l == DEPTH - 1 else None,
                               layer=l, sub=2, bsz=bsz)
            if name == 'ctx':
                new_states.append(jnp.stack([st_f, st_b], axis=0))
    y_prompt = jnp.transpose(streams['ctx']['x'].reshape(l_ctx, b_ctx, D_MODEL), (1, 0, 2))
    y_sample = jnp.transpose(streams['lat']['x'].reshape(l_lat, b_lat, D_MODEL), (1, 0, 2))
    new_state_lru = jnp.transpose(jnp.stack(new_states, axis=0), (2, 0, 1, 3))
    return (y_prompt, y_sample, new_state_lru)
```
